```python
import math
import jax, jax.numpy as jnp
from jax import lax
import numpy as np

D_MODEL = 1024
BATCH = 4
SEQ = 4096
DEPTH = 4
DEC_BATCH = 32
DEC_SEQ = 1
PAST_LEN = 8192
PAGE_SIZE = 128

N_A_LAYERS = DEPTH // 2
N_B_LAYERS = DEPTH - N_A_LAYERS
NORM_EPS = 1e-5
D_FF = 2816
SSM_EXPAND = 2
D_INNER = SSM_EXPAND * D_MODEL
SSM_HEAD_DIM = 64
SSM_HEADS = D_INNER // SSM_HEAD_DIM
SSM_GROUPS = 4
HEADS_PER_GROUP = SSM_HEADS // SSM_GROUPS
SSM_D_STATE = 128
SSM_CONV_W = 4
SSM_CONV_DIM = D_INNER + 2 * SSM_GROUPS * SSM_D_STATE
SSM_IN_DIM = D_INNER + SSM_CONV_DIM + SSM_HEADS
SSD_CHUNK = 128
DT_MIN = 0.001
DT_MAX = 0.1
N_HEADS = 8
HEAD_DIM = D_MODEL // N_HEADS
ROT_DIM = HEAD_DIM // 4
ROPE_THETA = 500000.0
MOBA_BLOCK = 256
MOBA_TOPK = 3
ATTN_Q_BLOCK = 64

kernel_name = "yoco_mamba2_moba_macaron_step"


def rms_norm(x, g):
    xf = x.astype(jnp.float32)
    y = xf * lax.rsqrt(jnp.mean(xf * xf, axis=-1, keepdims=True) + NORM_EPS)
    return (y * g.astype(jnp.float32)).astype(x.dtype)


def swiglu(x, w_up, w_down):
    gate, up = jnp.split(x @ w_up, 2, axis=-1)
    return (jax.nn.silu(gate) * up) @ w_down


def partial_rope(x, pos):
    half = ROT_DIM // 2
    inv = ROPE_THETA ** (-2.0 * jnp.arange(half, dtype=jnp.float32) / ROT_DIM)
    ang = pos.astype(jnp.float32)[:, None] * inv[None, :]
    cos = jnp.cos(ang)[None, :, None, :]
    sin = jnp.sin(ang)[None, :, None, :]
    xr = x[..., :ROT_DIM].astype(jnp.float32)
    x1, x2 = xr[..., :half], xr[..., half:]
    rot = jnp.concatenate([x1 * cos - x2 * sin, x2 * cos + x1 * sin], axis=-1)
    return jnp.concatenate([rot.astype(x.dtype), x[..., ROT_DIM:]], axis=-1)


def causal_depthwise_conv(conv_in, w):
    ch = conv_in.shape[-1]
    return lax.conv_general_dilated(conv_in, w[:, None, :].astype(conv_in.dtype), (1,), 'VALID',
                                    dimension_numbers=('NWC', 'WIO', 'NWC'), feature_group_count=ch)


def ssd_scan(x, a, b, c, h0):
    n, L = x.shape[0], x.shape[1]
    cs = min(SSD_CHUNK, L)
    pad = (-L) % cs
    lp = L + pad
    nc = lp // cs

    def chunked(t):
        t = jnp.pad(t, [(0, 0), (0, pad)] + [(0, 0)] * (t.ndim - 2))
        return t.reshape((n, nc, cs) + t.shape[2:])

    x, a, b, c = chunked(x), chunked(a), chunked(b), chunked(c)
    acum = jnp.cumsum(a, axis=2)
    causal = jnp.tril(jnp.ones((cs, cs), dtype=bool))[None, None, :, :, None, None]
    seg = acum[:, :, :, None] - acum[:, :, None, :]
    lmat = jnp.exp(jnp.where(causal, seg, -jnp.inf))
    cb = jnp.einsum('nclgk,ncsgk->nclsg', c, b)
    y_diag = jnp.einsum('nclsg,nclsge,ncsgep->nclgep', cb, lmat, x)
    decay = jnp.exp(acum[:, :, -1:] - acum)
    states = jnp.einsum('ncsgk,ncsge,ncsgep->ncgepk', b, decay, x)
    chunk_decay = jnp.exp(acum[:, :, -1])

    def body(h, inp):
        st_c, dec_c = inp
        return h * dec_c[..., None, None] + st_c, h

    h_last, h_in = lax.scan(body, h0, (jnp.moveaxis(states, 1, 0), jnp.moveaxis(chunk_decay, 1, 0)))
    h_in = jnp.moveaxis(h_in, 0, 1)
    y_off = jnp.einsum('nclgk,ncgepk,nclge->nclgep', c, h_in, jnp.exp(acum))
    y = (y_diag + y_off).reshape(n, lp, SSM_GROUPS, HEADS_PER_GROUP, SSM_HEAD_DIM)[:, :L]
    return y, h_last


def mamba2_mixer(u, conv_prev, ssm_prev, w_in, conv_w, conv_b, dt_bias, a_log, d_skip, norm_w, w_out):
    n, L, _ = u.shape
    zxbcdt = u @ w_in
    z = zxbcdt[..., :D_INNER]
    xbc = zxbcdt[..., D_INNER:D_INNER + SSM_CONV_DIM]
    dt = zxbcdt[..., D_INNER + SSM_CONV_DIM:]
    conv_in = jnp.concatenate([conv_prev.astype(u.dtype), xbc], axis=1)
    new_conv = conv_in[:, -(SSM_CONV_W - 1):]
    xbc = jax.nn.silu(causal_depthwise_conv(conv_in, conv_w) + conv_b.astype(u.dtype))
    gk = SSM_GROUPS * SSM_D_STATE
    xs = xbc[..., :D_INNER].reshape(n, L, SSM_GROUPS, HEADS_PER_GROUP, SSM_HEAD_DIM).astype(jnp.float32)
    bm = xbc[..., D_INNER:D_INNER + gk].reshape(n, L, SSM_GROUPS, SSM_D_STATE).astype(jnp.float32)
    cm = xbc[..., D_INNER + gk:].reshape(n, L, SSM_GROUPS, SSM_D_STATE).astype(jnp.float32)
    dt = jax.nn.softplus(dt.astype(jnp.float32) + dt_bias.astype(jnp.float32))
    dt = dt.reshape(n, L, SSM_GROUPS, HEADS_PER_GROUP)
    a = -jnp.exp(a_log.astype(jnp.float32)).reshape(SSM_GROUPS, HEADS_PER_GROUP)
    h0 = ssm_prev.astype(jnp.float32).reshape(n, SSM_GROUPS, HEADS_PER_GROUP, SSM_HEAD_DIM, SSM_D_STATE)
    y, h_last = ssd_scan(xs * dt[..., None], dt * a, bm, cm, h0)
    y = y + d_skip.astype(jnp.float32).reshape(SSM_GROUPS, HEADS_PER_GROUP)[..., None] * xs
    y = y.reshape(n, L, D_INNER)
    y = rms_norm(y * jax.nn.silu(z.astype(jnp.float32)), norm_w).astype(u.dtype)
    return y @ w_out, new_conv, h_last.reshape(n, SSM_HEADS, SSM_HEAD_DIM, SSM_D_STATE)


def moba_attention(q, k, v, q_offset):
    n_seq, lq = q.shape[0], q.shape[1]
    lk = k.shape[1]
    n_blk = -(-lk // MOBA_BLOCK)
    kpad = n_blk * MOBA_BLOCK - lk
    pad_spec = ((0, 0), (0, kpad), (0, 0), (0, 0))
    kb = jnp.pad(k, pad_spec).reshape(n_seq, n_blk, MOBA_BLOCK, N_HEADS, HEAD_DIM)
    vb = jnp.pad(v, pad_spec).reshape(n_seq, n_blk, MOBA_BLOCK, N_HEADS, HEAD_DIM)
    means = jnp.mean(kb.astype(jnp.float32), axis=2)
    n_sel = min(MOBA_TOPK, n_blk)
    qb = min(ATTN_Q_BLOCK, lq)
    n_qb = -(-lq // qb)
    qpad = n_qb * qb - lq
    q_steps = jnp.pad(q, ((0, 0), (0, qpad), (0, 0), (0, 0))).reshape(n_seq * n_qb, qb, N_HEADS, HEAD_DIM)
    seq_idx = jnp.repeat(jnp.arange(n_seq, dtype=jnp.int32), n_qb)
    starts = jnp.tile(q_offset + qb * jnp.arange(n_qb, dtype=jnp.int32), n_seq)
    head_idx = jnp.arange(N_HEADS)[None, :, None]
    scale = HEAD_DIM ** -0.5

    def step(args):
        qs, s_i, p0 = args
        qf = qs.astype(jnp.float32)
        pos = p0 + jnp.arange(qb, dtype=jnp.int32)
        own = p0 // MOBA_BLOCK
        kn, vn = kb[s_i], vb[s_i]
        gate = jnp.einsum('qhd,jhd->qhj', qf, means[s_i])
        gate = jnp.where(jnp.arange(n_blk) < own, gate, -jnp.inf)
        _, idx = lax.top_k(gate, n_sel)
        slot_ok = (jnp.arange(n_sel) < own)[:, None]
        k_sel = kn[idx, :, head_idx, :].astype(jnp.float32)
        v_sel = vn[idx, :, head_idx, :].astype(jnp.float32)
        s_sel = jnp.where(slot_ok, jnp.einsum('qhd,qhkbd->qhkb', qf, k_sel) * scale, -jnp.inf)
        k_own = lax.dynamic_index_in_dim(kn, own, 0, keepdims=False).astype(jnp.float32)
        v_own = lax.dynamic_index_in_dim(vn, own, 0, keepdims=False).astype(jnp.float32)
        kpos = own * MOBA_BLOCK + jnp.arange(MOBA_BLOCK, dtype=jnp.int32)
        s_own = jnp.einsum('qhd,bhd->qhb', qf, k_own) * scale
        s_own = jnp.where(kpos[None, None, :] <= pos[:, None, None], s_own, -jnp.inf)
        probs = jax.nn.softmax(jnp.concatenate([s_sel.reshape(qb, N_HEADS, n_sel * MOBA_BLOCK), s_own], axis=-1), axis=-1)
        p_sel = probs[..., :n_sel * MOBA_BLOCK].reshape(qb, N_HEADS, n_sel, MOBA_BLOCK)
        p_own = probs[..., n_sel * MOBA_BLOCK:]
        out = jnp.einsum('qhkb,qhkbd->qhd', p_sel, v_sel) + jnp.einsum('qhb,bhd->qhd', p_own, v_own)
        return out.astype(q.dtype)

    out = lax.map(step, (q_steps, seq_idx, starts))
    return out.reshape(n_seq, n_qb * qb, N_HEADS, HEAD_DIM)[:, :lq]


def run_trunk(h, pos0, conv_prev, ssm_prev, past_k, past_v,
              norm_ffn_a, w_ffn_a_up, w_ffn_a_down, norm_mix, norm_ffn_b, w_ffn_b_up, w_ffn_b_down,
              ssm_w_in, ssm_conv_w, ssm_conv_b, ssm_dt_bias, ssm_a_log, ssm_d, ssm_norm, ssm_w_out,
              norm_kv, w_kv, w_q, w_o, norm_final):
    n, L, _ = h.shape
    pos = pos0 + jnp.arange(L, dtype=jnp.int32)
    convs, ssms = [], []
    k_new = v_new = k_all = v_all = None
    for layer in range(DEPTH):
        if layer == N_A_LAYERS:
            kv = (rms_norm(h, norm_kv) @ w_kv).reshape(n, L, 2, N_HEADS, HEAD_DIM)
            k_new = partial_rope(kv[:, :, 0], pos)
            v_new = kv[:, :, 1]
            k_all = jnp.concatenate([past_k.astype(h.dtype), k_new], axis=1)
            v_all = jnp.concatenate([past_v.astype(h.dtype), v_new], axis=1)
        h = h + 0.5 * swiglu(rms_norm(h, norm_ffn_a[layer]), w_ffn_a_up[layer], w_ffn_a_down[layer])
        u = rms_norm(h, norm_mix[layer])
        if layer < N_A_LAYERS:
            y, c_new, s_new = mamba2_mixer(u, conv_prev[layer], ssm_prev[layer], ssm_w_in[layer], ssm_conv_w[layer],
                                           ssm_conv_b[layer], ssm_dt_bias[layer], ssm_a_log[layer], ssm_d[layer],
                                           ssm_norm[layer], ssm_w_out[layer])
            convs.append(c_new.astype(conv_prev.dtype))
            ssms.append(s_new.astype(ssm_prev.dtype))
        else:
            j = layer - N_A_LAYERS
            q = partial_rope((u @ w_q[j]).reshape(n, L, N_HEADS, HEAD_DIM), pos)
            o = moba_attention(q, k_all, v_all, pos0)
            y = o.reshape(n, L, N_HEADS * HEAD_DIM) @ w_o[j]
        h = h + y
        h = h + 0.5 * swiglu(rms_norm(h, norm_ffn_b[layer]), w_ffn_b_up[layer], w_ffn_b_down[layer])
    return rms_norm(h, norm_final), jnp.stack(convs), jnp.stack(ssms), k_new, v_new


def setup_inputs(seed: int = 0) -> dict:
    key = jax.random.key(seed)
    ks = iter(jax.random.split(key, 40))

    def nrm(shape, scale):
        return scale * jax.random.normal(next(ks), shape, jnp.float32)

    def gain(shape):
        return 1.0 + nrm(shape, 0.01)

    n_pages = PAST_LEN // PAGE_SIZE
    n_used = DEC_BATCH * n_pages
    n_pool = n_used + n_used // 4
    x_prompt = nrm((BATCH, SEQ, D_MODEL), 1.0)
    x_sample = nrm((DEC_BATCH, DEC_SEQ, D_MODEL), 1.0)
    state_conv = nrm((N_A_LAYERS, DEC_BATCH, SSM_CONV_W - 1, SSM_CONV_DIM), 1.0)
    state_ssm = nrm((N_A_LAYERS, DEC_BATCH, SSM_HEADS, SSM_HEAD_DIM, SSM_D_STATE), 0.1)
    cache_k = nrm((n_pool, PAGE_SIZE, N_HEADS, HEAD_DIM), 1.0)
    cache_v = nrm((n_pool, PAGE_SIZE, N_HEADS, HEAD_DIM), 1.0)
    page_table = jax.random.permutation(next(ks), n_pool)[:n_used].reshape(DEC_BATCH, n_pages).astype(jnp.int32)
    u_dt = jax.random.uniform(next(ks), (N_A_LAYERS, SSM_HEADS), jnp.float32)
    dt0 = jnp.exp(u_dt * (math.log(DT_MAX) - math.log(DT_MIN)) + math.log(DT_MIN))
    ssm_dt_bias = dt0 + jnp.log(-jnp.expm1(-dt0))
    ssm_a_log = jnp.log(jax.random.uniform(next(ks), (N_A_LAYERS, SSM_HEADS), jnp.float32, 1.0, 16.0))
    return {
        'x_prompt': x_prompt,
        'x_sample': x_sample,
        'state_conv': state_conv,
        'state_ssm': state_ssm,
        'cache_k': cache_k,
        'cache_v': cache_v,
        'page_table': page_table,
        'norm_ffn_a': gain((DEPTH, D_MODEL)),
        'w_ffn_a_up': nrm((DEPTH, D_MODEL, 2 * D_FF), D_MODEL ** -0.5),
        'w_ffn_a_down': nrm((DEPTH, D_FF, D_MODEL), D_FF ** -0.5),
        'norm_mix': gain((DEPTH, D_MODEL)),
        'norm_ffn_b': gain((DEPTH, D_MODEL)),
        'w_ffn_b_up': nrm((DEPTH, D_MODEL, 2 * D_FF), D_MODEL ** -0.5),
        'w_ffn_b_down': nrm((DEPTH, D_FF, D_MODEL), D_FF ** -0.5),
        'ssm_w_in': nrm((N_A_LAYERS, D_MODEL, SSM_IN_DIM), D_MODEL ** -0.5),
        'ssm_conv_w': nrm((N_A_LAYERS, SSM_CONV_W, SSM_CONV_DIM), SSM_CONV_W ** -0.5),
        'ssm_conv_b': nrm((N_A_LAYERS, SSM_CONV_DIM), 0.01),
        'ssm_dt_bias': ssm_dt_bias,
        'ssm_a_log': ssm_a_log,
        'ssm_d': gain((N_A_LAYERS, SSM_HEADS)),
        'ssm_norm': gain((N_A_LAYERS, D_INNER)),
        'ssm_w_out': nrm((N_A_LAYERS, D_INNER, D_MODEL), D_INNER ** -0.5),
        'norm_kv': gain((D_MODEL,)),
        'w_kv': nrm((D_MODEL, 2 * N_HEADS * HEAD_DIM), D_MODEL ** -0.5),
        'w_q': nrm((N_B_LAYERS, D_MODEL, N_HEADS * HEAD_DIM), D_MODEL ** -0.5),
        'w_o': nrm((N_B_LAYERS, N_HEADS * HEAD_DIM, D_MODEL), (N_HEADS * HEAD_DIM) ** -0.5),
        'norm_final': gain((D_MODEL,)),
    }


def reference(x_prompt, x_sample, state_conv, state_ssm, cache_k, cache_v, page_table,
              norm_ffn_a, w_ffn_a_up, w_ffn_a_down, norm_mix, norm_ffn_b, w_ffn_b_up, w_ffn_b_down,
              ssm_w_in, ssm_conv_w, ssm_conv_b, ssm_dt_bias, ssm_a_log, ssm_d, ssm_norm, ssm_w_out,
              norm_kv, w_kv, w_q, w_o, norm_final):
    weights = (norm_ffn_a, w_ffn_a_up, w_ffn_a_down, norm_mix, norm_ffn_b, w_ffn_b_up, w_ffn_b_down,
               ssm_w_in, ssm_conv_w, ssm_conv_b, ssm_dt_bias, ssm_a_log, ssm_d, ssm_norm, ssm_w_out,
               norm_kv, w_kv, w_q, w_o, norm_final)
    bp = x_prompt.shape[0]
    conv0 = jnp.zeros((N_A_LAYERS, bp, SSM_CONV_W - 1, SSM_CONV_DIM), state_conv.dtype)
    ssm0 = jnp.zeros((N_A_LAYERS, bp, SSM_HEADS, SSM_HEAD_DIM, SSM_D_STATE), state_ssm.dtype)
    empty = jnp.zeros((bp, 0, N_HEADS, HEAD_DIM), cache_k.dtype)
    y_prompt, conv_p, ssm_p, k_p, v_p = run_trunk(x_prompt, 0, conv0, ssm0, empty, empty, *weights)
    bs, n_pages = page_table.shape
    past_len = n_pages * PAGE_SIZE
    past_k = cache_k[page_table].reshape(bs, past_len, N_HEADS, HEAD_DIM)
    past_v = cache_v[page_table].reshape(bs, past_len, N_HEADS, HEAD_DIM)
    y_sample, conv_s, ssm_s, k_s, v_s = run_trunk(x_sample, past_len, state_conv, state_ssm, past_k, past_v, *weights)
    return (y_prompt, y_sample, conv_p, ssm_p, k_p, v_p, conv_s, ssm_s, k_s, v_s)
```

```python
import functools
import math

import jax
import jax.numpy as jnp
from jax import lax
from jax.experimental import pallas as pl
from jax.experimental.pallas import tpu as pltpu

F32 = jnp.float32
BF16 = jnp.bfloat16

D_MODEL = 1024
DEPTH = 4
N_A_LAYERS = 2
NORM_EPS = 1e-5
D_FF = 2816
D_INNER = 2048
SSM_HEAD_DIM = 64
SSM_HEADS = 32
SSM_GROUPS = 4
HEADS_PER_GROUP = 8
SSM_D_STATE = 128
SSM_CONV_W = 4
SSM_CONV_DIM = D_INNER + 2 * SSM_GROUPS * SSM_D_STATE
SSD_CHUNK = 128
GROUP_X = HEADS_PER_GROUP * SSM_HEAD_DIM
N_HEADS = 8
HEAD_DIM = 128
ROT_DIM = 32
ROPE_THETA = 500000.0
MOBA_BLOCK = 256
MOBA_TOPK = 3
PAGE_SIZE = 128
PAGES_PER_BLOCK = MOBA_BLOCK // PAGE_SIZE

LANES = 128
SUBLANES = 8
VMEM_LIMIT_BYTES = 56 * 1024 * 1024
FF_CHUNK = 256
NEG_INF = float("-inf")


def _cparams(*sem):
    return pltpu.CompilerParams(dimension_semantics=sem, vmem_limit_bytes=VMEM_LIMIT_BYTES)


def _resident(shape):
    nd = len(shape)
    return pl.BlockSpec(shape, lambda *_: (0,) * nd, pipeline_mode=pl.Buffered(1))


def _rms(x, g):
    return x * lax.rsqrt(jnp.mean(x * x, axis=-1, keepdims=True) + NORM_EPS) * g


def _silu(x):
    return x * jax.nn.sigmoid(x)


def _split3(x):
    hi = x.astype(BF16)
    r1 = x - hi.astype(F32)
    mid = r1.astype(BF16)
    lo = (r1 - mid.astype(F32)).astype(BF16)
    return hi, mid, lo


def _dot(a, b):
    return jnp.dot(a, b, preferred_element_type=F32)


def _dot_nt(a, b):
    return lax.dot_general(a, b, (((1,), (1,)), ((), ())), preferred_element_type=F32)


def _ffn_kernel(*refs, final_norm):
    if final_norm:
        x_ref, g_ref, wup_ref, wdn_ref, gf_ref, o_ref = refs
    else:
        x_ref, g_ref, wup_ref, wdn_ref, o_ref = refs
    x = x_ref[...]
    xn = _rms(x, g_ref[...]).astype(BF16)
    acc = jnp.zeros_like(x)
    for c in range(D_FF // FF_CHUNK):
        lo, hi = c * FF_CHUNK, (c + 1) * FF_CHUNK
        gate = _dot(xn, wup_ref[:, lo:hi])
        up = _dot(xn, wup_ref[:, D_FF + lo:D_FF + hi])
        act = (_silu(gate) * up).astype(BF16)
        acc = acc + _dot(act, wdn_ref[lo:hi, :])
    h = x + 0.5 * acc
    if final_norm:
        h = _rms(h, gf_ref[...])
    o_ref[...] = h


def _ffn(h, g, w_up, w_dn, tm, final_g=None):
    t, d = h.shape
    row = pl.BlockSpec((tm, d), lambda i: (i, 0))
    in_specs = [row, _resident((1, d)), _resident(w_up.shape), _resident(w_dn.shape)]
    args = [h, g.reshape(1, d), w_up, w_dn]
    if final_g is not None:
        in_specs.append(_resident((1, d)))
        args.append(final_g.reshape(1, d))
    return pl.pallas_call(
        functools.partial(_ffn_kernel, final_norm=final_g is not None),
        out_shape=jax.ShapeDtypeStruct((t, d), F32),
        grid=(t // tm,),
        in_specs=in_specs,
        out_specs=row,
        compiler_params=_cparams("parallel"),
        name="ffn",
    )(*args)


PROJ_CHUNK = 512


def _store_proj(xn, w_ref, o_ref, col0, width):
    for c0 in range(0, width, PROJ_CHUNK):
        c1 = min(c0 + PROJ_CHUNK, width)
        o_ref[:, c0:c1] = _dot(xn, w_ref[:, col0 + c0:col0 + c1]).astype(o_ref.dtype)


def _proj_kernel(x_ref, g_ref, w_ref, *o_refs):
    xn = _rms(x_ref[...], g_ref[...]).astype(BF16)
    col = 0
    for o_ref in o_refs:
        width = o_ref.shape[1]
        _store_proj(xn, w_ref, o_ref, col, width)
        col += width


def _proj(h, g, w, widths, tm):
    t, d = h.shape
    return pl.pallas_call(
        _proj_kernel,
        out_shape=[jax.ShapeDtypeStruct((t, n), F32) for n in widths],
        grid=(t // tm,),
        in_specs=[pl.BlockSpec((tm, d), lambda i: (i, 0)), _resident((1, d)), _resident(w.shape)],
        out_specs=[pl.BlockSpec((tm, n), lambda i: (i, 0)) for n in widths],
        compiler_params=_cparams("parallel"),
        name="proj",
    )(h, g.reshape(1, d), w)


def _rope(y, cos, sin_lo, sin_hi):
    outs = []
    for hd in range(y.shape[1] // HEAD_DIM):
        xh = y[:, hd * HEAD_DIM:(hd + 1) * HEAD_DIM]
        up = pltpu.roll(xh, HEAD_DIM - ROT_DIM // 2, axis=1)
        dn = pltpu.roll(xh, ROT_DIM // 2, axis=1)
        outs.append(xh * cos + up * sin_lo + dn * sin_hi)
    return jnp.concatenate(outs, axis=1)


def _rope_tables(pos):
    half = ROT_DIM // 2
    inv = ROPE_THETA ** (-2.0 * jnp.arange(half, dtype=F32) / ROT_DIM)
    ang = pos.astype(F32)[:, None] * inv[None, :]
    cos, sin = jnp.cos(ang), jnp.sin(ang)
    n = pos.shape[0]
    ones = jnp.ones((n, HEAD_DIM - ROT_DIM), F32)
    zeros = jnp.zeros((n, HEAD_DIM - half), F32)
    cos_t = jnp.concatenate([cos, cos, ones], axis=1)
    sin_lo = jnp.concatenate([-sin, zeros], axis=1)
    sin_hi = jnp.concatenate([jnp.zeros((n, half), F32), sin, jnp.zeros((n, HEAD_DIM - ROT_DIM), F32)], axis=1)
    return cos_t, sin_lo, sin_hi


def _q_kernel(x_ref, g_ref, w_ref, cos_ref, slo_ref, shi_ref, q_ref):
    xn = _rms(x_ref[...], g_ref[...]).astype(BF16)
    q_ref[...] = _rope(_dot(xn, w_ref[...]), cos_ref[...], slo_ref[...], shi_ref[...])


def _q_proj(h, g, w, tables, tm):
    t, d = h.shape
    n_tab = tables[0].shape[0] // tm
    tab = pl.BlockSpec((tm, HEAD_DIM), lambda i: (i % n_tab, 0))
    row = pl.BlockSpec((tm, d), lambda i: (i, 0))
    return pl.pallas_call(
        _q_kernel,
        out_shape=jax.ShapeDtypeStruct((t, d), F32),
        grid=(t // tm,),
        in_specs=[row, _resident((1, d)), _resident(w.shape), tab, tab, tab],
        out_specs=row,
        compiler_params=_cparams("parallel"),
        name="q_proj",
    )(h, g.reshape(1, d), w, *tables)


def _kv_kernel(x_ref, g_ref, w_ref, cos_ref, slo_ref, shi_ref, k_ref, v_ref, *ksum_ref, block_sums):
    xn = _rms(x_ref[...], g_ref[...]).astype(BF16)
    k = _rope(_dot(xn, w_ref[:, :D_MODEL]), cos_ref[...], slo_ref[...], shi_ref[...])
    k_ref[...] = k
    v_ref[...] = _dot(xn, w_ref[:, D_MODEL:])
    if block_sums:
        ksum_ref[0][0] = jnp.sum(k, axis=0, keepdims=True) * (1.0 / MOBA_BLOCK)


def _kv_proj(h, g, w, tables, tm, block_sums):
    t, d = h.shape
    n_tab = tables[0].shape[0] // tm
    tab = pl.BlockSpec((tm, HEAD_DIM), lambda i: (i % n_tab, 0))
    row = pl.BlockSpec((tm, d), lambda i: (i, 0))
    out_shape = [jax.ShapeDtypeStruct((t, d), F32), jax.ShapeDtypeStruct((t, d), F32)]
    out_specs = [row, row]
    if block_sums:
        assert tm == MOBA_BLOCK
        out_shape.append(jax.ShapeDtypeStruct((t // tm, 1, d), F32))
        out_specs.append(pl.BlockSpec((1, 1, d), lambda i: (i, 0, 0)))
    return pl.pallas_call(
        functools.partial(_kv_kernel, block_sums=block_sums),
        out_shape=out_shape,
        grid=(t // tm,),
        in_specs=[row, _resident((1, d)), _resident(w.shape), tab, tab, tab],
        out_specs=out_specs,
        compiler_params=_cparams("parallel"),
        name="kv_proj",
    )(h, g.reshape(1, d), w, *tables)


def _out_kernel(*refs, gated):
    if gated:
        y_ref, z_ref, nw_ref, w_ref, h_ref, o_ref = refs
        y = _rms(y_ref[...] * _silu(z_ref[...]), nw_ref[...]).astype(BF16)
    else:
        y_ref, w_ref, h_ref, o_ref = refs
        y = y_ref[...].astype(BF16)
    o_ref[...] = h_ref[...] + _dot(y, w_ref[...])


def _out_proj(y, w, h, tm, z=None, norm_w=None):
    t, d = h.shape
    k = y.shape[1]
    row_k = pl.BlockSpec((tm, k), lambda i: (i, 0))
    row_d = pl.BlockSpec((tm, d), lambda i: (i, 0))
    gated = z is not None
    if gated:
        in_specs = [row_k, row_k, _resident((1, k)), _resident(w.shape), row_d]
        args = (y, z, norm_w.reshape(1, k), w, h)
    else:
        in_specs = [row_k, _resident(w.shape), row_d]
        args = (y, w, h)
    return pl.pallas_call(
        functools.partial(_out_kernel, gated=gated),
        out_shape=jax.ShapeDtypeStruct((t, d), F32),
        grid=(t // tm,),
        in_specs=in_specs,
        out_specs=row_d,
        compiler_params=_cparams("parallel"),
        name="out_proj",
    )(*args)


def _conv_silu(ext_ref, cur, w, b, first):
    rows = cur.shape[0]

    @pl.when(first)
    def _():
        ext_ref[0:SUBLANES, :] = jnp.zeros((SUBLANES, cur.shape[1]), F32)

    ext_ref[SUBLANES:SUBLANES + rows, :] = cur
    acc = b
    for k in range(SSM_CONV_W):
        off = SUBLANES - (SSM_CONV_W - 1) + k
        acc = acc + w[k:k + 1, :] * ext_ref[off:off + rows, :]
    ext_ref[0:SUBLANES, :] = ext_ref[rows:rows + SUBLANES, :]
    return _silu(acc)


def _ssd_kernel(x_ref, b_ref, c_ref, dt_ref, wx_ref, wb_ref, wc_ref, bx_ref, bb_ref, bc_ref,
                dtb_ref, alog_ref, dsk_ref, y_ref, st_ref, xe_ref, be_ref, ce_ref, xw_ref):
    first = pl.program_id(2) == 0
    cs = SSD_CHUNK

    @pl.when(first)
    def _():
        st_ref[...] = jnp.zeros(st_ref.shape, F32)

    xs = _conv_silu(xe_ref, x_ref[0], wx_ref[...], bx_ref[...], first)
    bm = _conv_silu(be_ref, b_ref[0], wb_ref[...], bb_ref[...], first)
    cm = _conv_silu(ce_ref, c_ref[0], wc_ref[...], bc_ref[...], first)
    bm16, cm16 = bm.astype(BF16), cm.astype(BF16)

    dt = jax.nn.softplus(dt_ref[...] + dtb_ref[0])
    a = dt * (-jnp.exp(alog_ref[0]))
    row = lax.broadcasted_iota(jnp.int32, (cs, cs), 0)
    col = lax.broadcasted_iota(jnp.int32, (cs, cs), 1)
    causal = row >= col
    tril = jnp.where(causal, 1.0, 0.0).astype(BF16)
    a_hi, a_mid, a_lo = _split3(a)
    acum = _dot(tril, a_hi) + _dot(tril, a_mid) + _dot(tril, a_lo)
    acum_t = acum.T
    dsk = dsk_ref[0]

    cb = _dot_nt(cm16, bm16)
    st_old = st_ref[0].reshape(GROUP_X, SSM_D_STATE)
    y_off = _dot_nt(cm16, st_old.astype(BF16))

    for e in range(HEADS_PER_GROUP):
        lo, hi = e * SSM_HEAD_DIM, (e + 1) * SSM_HEAD_DIM
        ac_col = acum[:, e:e + 1]
        ac_row = acum_t[e:e + 1, :]
        ac_last = acum[cs - 1:cs, e:e + 1]
        lmat = jnp.exp(jnp.where(causal, ac_col - ac_row, NEG_INF))
        xs_e = xs[:, lo:hi]
        xdt = xs_e * dt[:, e:e + 1]
        y_diag = _dot((cb * lmat).astype(BF16), xdt.astype(BF16))
        y_ref[0, :, lo:hi] = y_diag + y_off[:, lo:hi] * jnp.exp(ac_col) + dsk[:, e:e + 1] * xs_e
        xw_ref[:, lo:hi] = xdt * jnp.exp(ac_last - ac_col)

    states = _dot(xw_ref[...].T.astype(BF16), bm16)
    for e in range(HEADS_PER_GROUP):
        lo, hi = e * SSM_HEAD_DIM, (e + 1) * SSM_HEAD_DIM
        chunk_decay = jnp.exp(acum[cs - 1:cs, e:e + 1])
        st_ref[0, e] = st_ref[0, e] * chunk_decay + states[lo:hi, :]


def _group_lanes(v):
    v = v.astype(F32).reshape(SSM_GROUPS, 1, HEADS_PER_GROUP)
    return jnp.pad(v, ((0, 0), (0, 0), (0, LANES - HEADS_PER_GROUP)))


def _ssd_prompt(xbc, dt, conv_w, conv_b, dt_bias, a_log, d_skip, n_seq, seq):
    nc = seq // SSD_CHUNK
    cs = SSD_CHUNK
    xbc = xbc.reshape(n_seq, seq, SSM_CONV_DIM)
    nx = D_INNER // GROUP_X
    nb = D_INNER // SSM_D_STATE
    conv_b = conv_b.reshape(1, SSM_CONV_DIM)
    par = _group_lanes

    def lanes(width, idx):
        return pl.BlockSpec((1, cs, width), idx)

    in_specs = [
        lanes(GROUP_X, lambda n, g, c: (n, c, g)),
        lanes(SSM_D_STATE, lambda n, g, c: (n, c, nb + g)),
        lanes(SSM_D_STATE, lambda n, g, c: (n, c, nb + SSM_GROUPS + g)),
        pl.BlockSpec((cs, LANES), lambda n, g, c: (n * nc + c, g)),
        pl.BlockSpec((SSM_CONV_W, GROUP_X), lambda n, g, c: (0, g)),
        pl.BlockSpec((SSM_CONV_W, SSM_D_STATE), lambda n, g, c: (0, nb + g)),
        pl.BlockSpec((SSM_CONV_W, SSM_D_STATE), lambda n, g, c: (0, nb + SSM_GROUPS + g)),
        pl.BlockSpec((1, GROUP_X), lambda n, g, c: (0, g)),
        pl.BlockSpec((1, SSM_D_STATE), lambda n, g, c: (0, nb + g)),
        pl.BlockSpec((1, SSM_D_STATE), lambda n, g, c: (0, nb + SSM_GROUPS + g)),
        pl.BlockSpec((1, 1, LANES), lambda n, g, c: (g, 0, 0)),
        pl.BlockSpec((1, 1, LANES), lambda n, g, c: (g, 0, 0)),
        pl.BlockSpec((1, 1, LANES), lambda n, g, c: (g, 0, 0)),
    ]
    del nx
    return pl.pallas_call(
        _ssd_kernel,
        out_shape=[jax.ShapeDtypeStruct((n_seq, seq, D_INNER), F32),
                   jax.ShapeDtypeStruct((n_seq, SSM_HEADS, SSM_HEAD_DIM, SSM_D_STATE), F32)],
        grid=(n_seq, SSM_GROUPS, nc),
        in_specs=in_specs,
        out_specs=[pl.BlockSpec((1, cs, GROUP_X), lambda n, g, c: (n, c, g)),
                   pl.BlockSpec((1, HEADS_PER_GROUP, SSM_HEAD_DIM, SSM_D_STATE), lambda n, g, c: (n, g, 0, 0))],
        scratch_shapes=[pltpu.VMEM((cs + SUBLANES, GROUP_X), F32),
                        pltpu.VMEM((cs + SUBLANES, SSM_D_STATE), F32),
                        pltpu.VMEM((cs + SUBLANES, SSM_D_STATE), F32),
                        pltpu.VMEM((cs, GROUP_X), F32)],
        compiler_params=_cparams("parallel", "parallel", "arbitrary"),
        name="ssd_prompt",
    )(xbc, xbc, xbc, dt, conv_w, conv_w, conv_w, conv_b, conv_b, conv_b,
      par(dt_bias), par(a_log), par(d_skip))


def _ssm_step_kernel(xbc_ref, cst_ref, dt_ref, w_ref, b_ref, dtb_ref, alog_ref, dsk_ref, st_ref,
                     y_ref, sto_ref):
    w = w_ref[...]
    acc = b_ref[...] + w[SSM_CONV_W - 1:SSM_CONV_W, :] * xbc_ref[0]
    for k in range(SSM_CONV_W - 1):
        acc = acc + w[k:k + 1, :] * cst_ref[0, k:k + 1, :]
    act = _silu(acc)
    nb = D_INNER
    row0 = lax.broadcasted_iota(jnp.int32, (SUBLANES, 1), 0) == 0
    hp = lax.Precision.HIGHEST
    for g in range(SSM_GROUPS):
        xs = act[:, g * GROUP_X:(g + 1) * GROUP_X]
        bm = act[:, nb + g * SSM_D_STATE:nb + (g + 1) * SSM_D_STATE]
        cm = act[:, nb + (SSM_GROUPS + g) * SSM_D_STATE:nb + (SSM_GROUPS + g + 1) * SSM_D_STATE]
        dt = jax.nn.softplus(dt_ref[0, :, g * LANES:(g + 1) * LANES] + dtb_ref[g])
        decay = jnp.exp(dt * (-jnp.exp(alog_ref[g])))
        xdt = jnp.concatenate(
            [xs[:, e * SSM_HEAD_DIM:(e + 1) * SSM_HEAD_DIM] * dt[:, e:e + 1] for e in range(HEADS_PER_GROUP)], axis=1)
        x8 = jnp.where(row0, xdt, 0.0)
        b8 = jnp.where(row0, bm, 0.0)
        c8 = jnp.where(row0, cm, 0.0)
        outer = lax.dot_general(x8, b8, (((0,), (0,)), ((), ())), precision=hp,
                                preferred_element_type=F32)
        new = []
        for e in range(HEADS_PER_GROUP):
            head = g * HEADS_PER_GROUP + e
            h_new = st_ref[0, head] * decay[:, e:e + 1] + outer[e * SSM_HEAD_DIM:(e + 1) * SSM_HEAD_DIM, :]
            sto_ref[0, head] = h_new
            new.append(h_new)
        h_g = jnp.concatenate(new, axis=0)
        y8 = lax.dot_general(c8, h_g, (((1,), (1,)), ((), ())), precision=hp,
                             preferred_element_type=F32)
        dsk = dsk_ref[g]
        dx = jnp.concatenate(
            [xs[:, e * SSM_HEAD_DIM:(e + 1) * SSM_HEAD_DIM] * dsk[:, e:e + 1] for e in range(HEADS_PER_GROUP)], axis=1)
        y_ref[0, :, g * GROUP_X:(g + 1) * GROUP_X] = y8[0:1, :] + dx


def _ssm_step(xbc, dt, conv_state, ssm_state, conv_w, conv_b, dt_bias, a_log, d_skip):
    n = xbc.shape[0]
    par = _group_lanes
    st_spec = pl.BlockSpec((1, SSM_HEADS, SSM_HEAD_DIM, SSM_D_STATE), lambda s: (s, 0, 0, 0))
    y, st = pl.pallas_call(
        _ssm_step_kernel,
        out_shape=[jax.ShapeDtypeStruct((n, 1, D_INNER), F32),
                   jax.ShapeDtypeStruct(ssm_state.shape, F32)],
        grid=(n,),
        in_specs=[pl.BlockSpec((1, 1, SSM_CONV_DIM), lambda s: (s, 0, 0)),
                  pl.BlockSpec((1, SSM_CONV_W - 1, SSM_CONV_DIM), lambda s: (s, 0, 0)),
                  pl.BlockSpec((1, 1, SSM_GROUPS * LANES), lambda s: (s, 0, 0)),
                  _resident((SSM_CONV_W, SSM_CONV_DIM)),
                  _resident((1, SSM_CONV_DIM)),
                  _resident((SSM_GROUPS, 1, LANES)),
                  _resident((SSM_GROUPS, 1, LANES)),
                  _resident((SSM_GROUPS, 1, LANES)),
                  st_spec],
        out_specs=[pl.BlockSpec((1, 1, D_INNER), lambda s: (s, 0, 0)), st_spec],
        compiler_params=_cparams("parallel"),
        name="ssm_step",
    )(xbc.reshape(n, 1, SSM_CONV_DIM), conv_state, dt.reshape(n, 1, SSM_GROUPS * LANES), conv_w,
      conv_b.reshape(1, SSM_CONV_DIM), par(dt_bias), par(a_log), par(d_skip), ssm_state)
    return y.reshape(n, D_INNER), st


def _moba_kernel(q_ref, k_ref, v_ref, mean_ref, o_ref, bias_ref):
    i = pl.program_id(2)
    blk = MOBA_BLOCK
    n_blk = mean_ref.shape[1]
    q = q_ref[0]
    gate = lax.dot_general(q, mean_ref[0], (((1,), (1,)), ((), ())), precision=lax.Precision.HIGHEST,
                           preferred_element_type=F32)
    lane = lax.broadcasted_iota(jnp.int32, (blk, n_blk), 1)
    gate = jnp.where(lane < i, gate, NEG_INF)
    own_col = jnp.full((blk, 1), i, jnp.int32)
    for j in range(n_blk):
        g_j = gate[:, j:j + 1]
        beats = (gate > g_j) | ((gate == g_j) & (lane < j))
        rank = jnp.sum(jnp.where(beats, 1.0, 0.0), axis=1, keepdims=True)
        chosen = (rank < MOBA_TOPK) & (own_col > j)
        bias_ref[j] = jnp.broadcast_to(jnp.where(chosen, 0.0, NEG_INF), (blk, LANES))

    qs = (q * (HEAD_DIM ** -0.5)).astype(BF16)
    own = pl.multiple_of(i * blk, blk)
    s = _dot_nt(qs, k_ref[0, pl.ds(own, blk), :])
    r = lax.broadcasted_iota(jnp.int32, (blk, blk), 0)
    c = lax.broadcasted_iota(jnp.int32, (blk, blk), 1)
    s = jnp.where(c <= r, s, NEG_INF)
    m = jnp.max(s, axis=1, keepdims=True)
    p = jnp.exp(s - m)
    l = jnp.sum(p, axis=1, keepdims=True)
    acc = _dot(p.astype(BF16), v_ref[0, pl.ds(own, blk), :])

    def body(j, carry):
        m, l, acc = carry
        off = pl.multiple_of(j * blk, blk)
        bias = bias_ref[j]
        s = _dot_nt(qs, k_ref[0, pl.ds(off, blk), :]) + jnp.concatenate([bias] * (blk // LANES), axis=1)
        m_new = jnp.maximum(m, jnp.max(s, axis=1, keepdims=True))
        alpha = jnp.exp(m - m_new)
        p = jnp.exp(s - m_new)
        l = alpha * l + jnp.sum(p, axis=1, keepdims=True)
        acc = alpha * acc + _dot(p.astype(BF16), v_ref[0, pl.ds(off, blk), :])
        return m_new, l, acc

    m, l, acc = lax.fori_loop(0, i, body, (m, l, acc))
    o_ref[0] = (acc / l).astype(o_ref.dtype)


def _moba_prompt(q, k16, v16, kmean, n_seq, seq):
    n_blk = seq // MOBA_BLOCK
    tile = pl.BlockSpec((1, MOBA_BLOCK, HEAD_DIM), lambda n, h, i: (n, i, h))
    full = pl.BlockSpec((1, seq, HEAD_DIM), lambda n, h, i: (n, 0, h))
    return pl.pallas_call(
        _moba_kernel,
        out_shape=jax.ShapeDtypeStruct((n_seq, seq, D_MODEL), BF16),
        grid=(n_seq, N_HEADS, n_blk),
        in_specs=[tile, full, full, pl.BlockSpec((1, n_blk, HEAD_DIM), lambda n, h, i: (n, 0, h))],
        out_specs=tile,
        scratch_shapes=[pltpu.VMEM((n_blk, MOBA_BLOCK, LANES), F32)],
        compiler_params=_cparams("parallel", "parallel", "arbitrary"),
        name="moba_prompt",
    )(q, k16, v16, kmean)


def _page_mean_kernel(pt_ref, p0_ref, p1_ref, o_ref):
    del pt_ref
    o_ref[0, 0] = (jnp.sum(p0_ref[0], axis=0, keepdims=True)
                   + jnp.sum(p1_ref[0], axis=0, keepdims=True)) * (1.0 / MOBA_BLOCK)


def _page_means(cache_k, page_table):
    n_seq, n_pages = page_table.shape
    n_blk = n_pages // PAGES_PER_BLOCK
    pages = cache_k.reshape(cache_k.shape[0], PAGE_SIZE, D_MODEL)
    page = lambda t: pl.BlockSpec(
        (1, PAGE_SIZE, D_MODEL), lambda s, b, pt: (pt[s * n_pages + PAGES_PER_BLOCK * b + t], 0, 0))
    out = pl.pallas_call(
        _page_mean_kernel,
        out_shape=jax.ShapeDtypeStruct((n_seq, n_blk, 1, D_MODEL), F32),
        grid_spec=pltpu.PrefetchScalarGridSpec(
            num_scalar_prefetch=1,
            grid=(n_seq, n_blk),
            in_specs=[page(0), page(1)],
            out_specs=pl.BlockSpec((1, 1, 1, D_MODEL), lambda s, b, pt: (s, b, 0, 0)),
        ),
        compiler_params=_cparams("parallel", "parallel"),
        name="page_means",
    )(page_table.reshape(-1), pages, pages)
    return out.reshape(n_seq, n_blk, D_MODEL)


def _select_kernel(q_ref, mean_ref, idx_ref):
    n_blk = mean_ref.shape[1]
    prod = mean_ref[0] * q_ref[0]
    lane = lax.broadcasted_iota(jnp.int32, (n_blk, LANES), 1)
    sub = lax.broadcasted_iota(jnp.int32, (n_blk, LANES), 0)
    gate = jnp.zeros((n_blk, LANES), F32)
    for hd in range(N_HEADS):
        g_h = jnp.sum(prod[:, hd * HEAD_DIM:(hd + 1) * HEAD_DIM], axis=1, keepdims=True)
        gate = jnp.where(lane == hd, g_h, gate)
    out_sub = lax.broadcasted_iota(jnp.int32, (SUBLANES, LANES), 0)
    out = jnp.zeros((SUBLANES, LANES), jnp.int32)
    for j in range(n_blk):
        g_j = gate[j:j + 1, :]
        beats = (gate > g_j) | ((gate == g_j) & (sub < j))
        rank = jnp.sum(jnp.where(beats, 1.0, 0.0), axis=0, keepdims=True)
        for r in range(MOBA_TOPK):
            out = jnp.where((out_sub == r) & (rank == float(r)), j, out)
    idx_ref[0] = out


def _select_blocks(q, means):
    n_seq, n_blk, _ = means.shape
    idx = pl.pallas_call(
        _select_kernel,
        out_shape=jax.ShapeDtypeStruct((n_seq, SUBLANES, LANES), jnp.int32),
        grid=(n_seq,),
        in_specs=[pl.BlockSpec((1, 1, D_MODEL), lambda s: (s, 0, 0)),
                  pl.BlockSpec((1, n_blk, D_MODEL), lambda s: (s, 0, 0))],
        out_specs=pl.BlockSpec((1, SUBLANES, LANES), lambda s: (s, 0, 0)),
        compiler_params=_cparams("parallel"),
        name="select_blocks",
    )(q.reshape(n_seq, 1, D_MODEL), means)
    return idx[:, :MOBA_TOPK, :N_HEADS]


N_SEL_PAGES = MOBA_TOPK * PAGES_PER_BLOCK


def _decode_attn_kernel(pt_ref, idx_ref, q_ref, kn_ref, vn_ref, *refs):
    del pt_ref, idx_ref
    k_refs, v_refs, o_ref = refs[:N_SEL_PAGES], refs[N_SEL_PAGES:2 * N_SEL_PAGES], refs[2 * N_SEL_PAGES]
    q = q_ref[0] * (HEAD_DIM ** -0.5)
    s_own = jnp.sum(q * kn_ref[0], axis=1, keepdims=True)
    scores = [jnp.sum(k_ref[0] * q, axis=1, keepdims=True) for k_ref in k_refs]
    m = s_own
    for s in scores:
        m = jnp.maximum(m, jnp.max(s, axis=0, keepdims=True))
    p_own = jnp.exp(s_own - m)
    l = p_own
    acc = p_own * vn_ref[0]
    for s, v_ref in zip(scores, v_refs):
        p = jnp.exp(s - m)
        l = l + jnp.sum(p, axis=0, keepdims=True)
        acc = acc + jnp.sum(p * v_ref[0], axis=0, keepdims=True)
    o_ref[0] = acc / l


def _decode_attn(q, k_new, v_new, cache_k, cache_v, page_table, idx):
    n_seq, n_pages = page_table.shape
    kp = cache_k.reshape(cache_k.shape[0], PAGE_SIZE, D_MODEL)
    vp = cache_v.reshape(cache_v.shape[0], PAGE_SIZE, D_MODEL)

    def page(slot):
        r, t = divmod(slot, PAGES_PER_BLOCK)

        def index(s, h, pt, sel):
            b = sel[(s * MOBA_TOPK + r) * N_HEADS + h]
            return (pt[s * n_pages + PAGES_PER_BLOCK * b + t], 0, h)
        return pl.BlockSpec((1, PAGE_SIZE, HEAD_DIM), index)

    vec = pl.BlockSpec((1, 1, HEAD_DIM), lambda s, h, pt, sel: (s, 0, h))
    pages = [page(slot) for slot in range(N_SEL_PAGES)]
    r3 = lambda a: a.reshape(n_seq, 1, D_MODEL)
    out = pl.pallas_call(
        _decode_attn_kernel,
        out_shape=jax.ShapeDtypeStruct((n_seq, 1, D_MODEL), F32),
        grid_spec=pltpu.PrefetchScalarGridSpec(
            num_scalar_prefetch=2,
            grid=(n_seq, N_HEADS),
            in_specs=[vec, vec, vec] + pages + pages,
            out_specs=vec,
        ),
        compiler_params=_cparams("parallel", "parallel"),
        name="decode_attn",
    )(page_table.reshape(-1), idx.reshape(-1), r3(q), r3(k_new), r3(v_new),
      *([kp] * N_SEL_PAGES), *([vp] * N_SEL_PAGES))
    return out.reshape(n_seq, D_MODEL)


def _in_proj_weight(w_in):
    w_z = w_in[:, :D_INNER]
    w_xbc = w_in[:, D_INNER:D_INNER + SSM_CONV_DIM]
    w_dt = w_in[:, D_INNER + SSM_CONV_DIM:].reshape(D_MODEL, SSM_GROUPS, HEADS_PER_GROUP)
    w_dt = jnp.pad(w_dt, ((0, 0), (0, 0), (0, LANES - HEADS_PER_GROUP))).reshape(D_MODEL, SSM_GROUPS * LANES)
    return jnp.concatenate([w_z, w_xbc, w_dt], axis=1).astype(BF16)


IN_WIDTHS = (D_INNER, SSM_CONV_DIM, SSM_GROUPS * LANES)


def kernel(x_prompt, x_sample, state_conv, state_ssm, cache_k, cache_v, page_table, norm_ffn_a, w_ffn_a_up, w_ffn_a_down, norm_mix, norm_ffn_b, w_ffn_b_up, w_ffn_b_down, ssm_w_in, ssm_conv_w, ssm_conv_b, ssm_dt_bias, ssm_a_log, ssm_d, ssm_norm, ssm_w_out, norm_kv, w_kv, w_q, w_o, norm_final):
    n_p, seq, d = x_prompt.shape
    n_s = x_sample.shape[0]
    past_len = page_table.shape[1] * PAGE_SIZE
    tm_p = 512
    tm_s = n_s

    hp = x_prompt.reshape(n_p * seq, d)
    hs = x_sample.reshape(n_s, d)
    tab_p = _rope_tables(jnp.arange(seq, dtype=jnp.int32))
    tab_s = _rope_tables(jnp.full((n_s,), past_len, jnp.int32))

    conv_p, ssm_p, conv_s, ssm_s = [], [], [], []
    k_p = v_p = k_s = v_s = None
    for layer in range(DEPTH):
        if layer == N_A_LAYERS:
            w_kv16 = w_kv.astype(BF16)
            k_p, v_p, kmean_p = _kv_proj(hp, norm_kv, w_kv16, tab_p, MOBA_BLOCK, True)
            k_s, v_s = _kv_proj(hs, norm_kv, w_kv16, tab_s, tm_s, False)
            k16 = k_p.astype(BF16).reshape(n_p, seq, d)
            v16 = v_p.astype(BF16).reshape(n_p, seq, d)
            kmean_p = kmean_p.reshape(n_p, seq // MOBA_BLOCK, d)
            kmean_s = _page_means(cache_k, page_table)

        w_up, w_dn = w_ffn_a_up[layer].astype(BF16), w_ffn_a_down[layer].astype(BF16)
        hp = _ffn(hp, norm_ffn_a[layer], w_up, w_dn, tm_p)
        hs = _ffn(hs, norm_ffn_a[layer], w_up, w_dn, tm_s)

        if layer < N_A_LAYERS:
            w_in = _in_proj_weight(ssm_w_in[layer])
            w_out = ssm_w_out[layer].astype(BF16)
            ssm_args = (ssm_conv_w[layer], ssm_conv_b[layer], ssm_dt_bias[layer], ssm_a_log[layer], ssm_d[layer])
            z, xbc, dt = _proj(hp, norm_mix[layer], w_in, IN_WIDTHS, tm_p)
            y, st = _ssd_prompt(xbc, dt, *ssm_args, n_p, seq)
            conv_p.append(xbc.reshape(n_p, seq, SSM_CONV_DIM)[:, seq - (SSM_CONV_W - 1):])
            ssm_p.append(st)
            hp = _out_proj(y.reshape(n_p * seq, D_INNER), w_out, hp, tm_p, z=z, norm_w=ssm_norm[layer])

            z, xbc, dt = _proj(hs, norm_mix[layer], w_in, IN_WIDTHS, tm_s)
            y, st = _ssm_step(xbc, dt, state_conv[layer], state_ssm[layer], *ssm_args)
            conv_s.append(jnp.concatenate([state_conv[layer][:, 1:], xbc[:, None, :]], axis=1))
            ssm_s.append(st)
            hs = _out_proj(y, w_out, hs, tm_s, z=z, norm_w=ssm_norm[layer])
        else:
            j = layer - N_A_LAYERS
            w_q16, w_o16 = w_q[j].astype(BF16), w_o[j].astype(BF16)
            q = _q_proj(hp, norm_mix[layer], w_q16, tab_p, tm_p)
            o = _moba_prompt(q.reshape(n_p, seq, d), k16, v16, kmean_p, n_p, seq)
            hp = _out_proj(o.reshape(n_p * seq, d), w_o16, hp, tm_p)

            q = _q_proj(hs, norm_mix[layer], w_q16, tab_s, tm_s)
            idx = _select_blocks(q, kmean_s)
            o = _decode_attn(q, k_s, v_s, cache_k, cache_v, page_table, idx)
            hs = _out_proj(o, w_o16, hs, tm_s)

        w_up, w_dn = w_ffn_b_up[layer].astype(BF16), w_ffn_b_down[layer].astype(BF16)
        final_g = norm_final if layer == DEPTH - 1 else None
        hp = _ffn(hp, norm_ffn_b[layer], w_up, w_dn, tm_p, final_g)
        hs = _ffn(hs, norm_ffn_b[layer], w_up, w_dn, tm_s, final_g)

    heads = (N_HEADS, HEAD_DIM)
    return (hp.reshape(n_p, seq, d), hs.reshape(n_s, 1, d),
            jnp.stack(conv_p), jnp.stack(ssm_p),
            k_p.reshape(n_p, seq, *heads), v_p.reshape(n_p, seq, *heads),
            jnp.stack(conv_s), jnp.stack(ssm_s),
            k_s.reshape(n_s, 1, *heads), v_s.reshape(n_s, 1, *heads))
```

```python
import functools
import math

import jax
import jax.numpy as jnp
from jax import lax
from jax.experimental import pallas as pl
from jax.experimental.pallas import tpu as pltpu

F32 = jnp.float32
BF16 = jnp.bfloat16

D_MODEL = 1024
DEPTH = 4
N_A_LAYERS = 2
NORM_EPS = 1e-5
D_FF = 2816
D_INNER = 2048
SSM_HEAD_DIM = 64
SSM_HEADS = 32
SSM_GROUPS = 4
HEADS_PER_GROUP = 8
SSM_D_STATE = 128
SSM_CONV_W = 4
SSM_CONV_DIM = D_INNER + 2 * SSM_GROUPS * SSM_D_STATE
SSD_CHUNK = 128
GROUP_X = HEADS_PER_GROUP * SSM_HEAD_DIM
N_HEADS = 8
HEAD_DIM = 128
ROT_DIM = 32
ROPE_THETA = 500000.0
MOBA_BLOCK = 256
MOBA_TOPK = 3
PAGE_SIZE = 128
PAGES_PER_BLOCK = MOBA_BLOCK // PAGE_SIZE

LANES = 128
SUBLANES = 8
VMEM_LIMIT_BYTES = 56 * 1024 * 1024
FF_CHUNK = 256
NEG_INF = float("-inf")


def _cparams(*sem):
    return pltpu.CompilerParams(dimension_semantics=sem, vmem_limit_bytes=VMEM_LIMIT_BYTES)


def _resident(shape):
    nd = len(shape)
    return pl.BlockSpec(shape, lambda *_: (0,) * nd, pipeline_mode=pl.Buffered(1))


def _rms(x, g):
    return x * lax.rsqrt(jnp.mean(x * x, axis=-1, keepdims=True) + NORM_EPS) * g


def _silu(x):
    return x * jax.nn.sigmoid(x)


def _split3(x):
    hi = x.astype(BF16)
    r1 = x - hi.astype(F32)
    mid = r1.astype(BF16)
    lo = (r1 - mid.astype(F32)).astype(BF16)
    return hi, mid, lo


def _dot(a, b):
    return jnp.dot(a, b, preferred_element_type=F32)


def _dot_nt(a, b):
    return lax.dot_general(a, b, (((1,), (1,)), ((), ())), preferred_element_type=F32)


def _ffn_kernel(*refs, final_norm):
    if final_norm:
        x_ref, g_ref, wup_ref, wdn_ref, gf_ref, o_ref = refs
    else:
        x_ref, g_ref, wup_ref, wdn_ref, o_ref = refs
    x = x_ref[...]
    xn = _rms(x, g_ref[...]).astype(BF16)
    acc = jnp.zeros_like(x)
    for c in range(D_FF // FF_CHUNK):
        lo, hi = c * FF_CHUNK, (c + 1) * FF_CHUNK
        gate = _dot(xn, wup_ref[:, lo:hi])
        up = _dot(xn, wup_ref[:, D_FF + lo:D_FF + hi])
        act = (_silu(gate) * up).astype(BF16)
        acc = acc + _dot(act, wdn_ref[lo:hi, :])
    h = x + 0.5 * acc
    if final_norm:
        h = _rms(h, gf_ref[...])
    o_ref[...] = h


def _ffn(h, g, w_up, w_dn, tm, final_g=None):
    t, d = h.shape
    row = pl.BlockSpec((tm, d), lambda i: (i, 0))
    in_specs = [row, _resident((1, d)), _resident(w_up.shape), _resident(w_dn.shape)]
    args = [h, g.reshape(1, d), w_up, w_dn]
    if final_g is not None:
        in_specs.append(_resident((1, d)))
        args.append(final_g.reshape(1, d))
    return pl.pallas_call(
        functools.partial(_ffn_kernel, final_norm=final_g is not None),
        out_shape=jax.ShapeDtypeStruct((t, d), F32),
        grid=(t // tm,),
        in_specs=in_specs,
        out_specs=row,
        compiler_params=_cparams("parallel"),
        name="ffn",
    )(*args)


PROJ_CHUNK = 512


def _store_proj(xn, w_ref, o_ref, col0, width):
    for c0 in range(0, width, PROJ_CHUNK):
        c1 = min(c0 + PROJ_CHUNK, width)
        o_ref[:, c0:c1] = _dot(xn, w_ref[:, col0 + c0:col0 + c1]).astype(o_ref.dtype)


def _proj_kernel(x_ref, g_ref, w_ref, *o_refs):
    xn = _rms(x_ref[...], g_ref[...]).astype(BF16)
    col = 0
    for o_ref in o_refs:
        width = o_ref.shape[1]
        _store_proj(xn, w_ref, o_ref, col, width)
        col += width


def _proj(h, g, w, widths, tm):
    t, d = h.shape
    return pl.pallas_call(
        _proj_kernel,
        out_shape=[jax.ShapeDtypeStruct((t, n), F32) for n in widths],
        grid=(t // tm,),
        in_specs=[pl.BlockSpec((tm, d), lambda i: (i, 0)), _resident((1, d)), _resident(w.shape)],
        out_specs=[pl.BlockSpec((tm, n), lambda i: (i, 0)) for n in widths],
        compiler_params=_cparams("parallel"),
        name="proj",
    )(h, g.reshape(1, d), w)


def _rope(y, cos, sin_lo, sin_hi):
    outs = []
    for hd in range(y.shape[1] // HEAD_DIM):
        xh = y[:, hd * HEAD_DIM:(hd + 1) * HEAD_DIM]
        up = pltpu.roll(xh, HEAD_DIM - ROT_DIM // 2, axis=1)
        dn = pltpu.roll(xh, ROT_DIM // 2, axis=1)
        outs.append(xh * cos + up * sin_lo + dn * sin_hi)
    return jnp.concatenate(outs, axis=1)


def _rope_tables(pos):
    half = ROT_DIM // 2
    inv = ROPE_THETA ** (-2.0 * jnp.arange(half, dtype=F32) / ROT_DIM)
    ang = pos.astype(F32)[:, None] * inv[None, :]
    cos, sin = jnp.cos(ang), jnp.sin(ang)
    n = pos.shape[0]
    ones = jnp.ones((n, HEAD_DIM - ROT_DIM), F32)
    zeros = jnp.zeros((n, HEAD_DIM - half), F32)
    cos_t = jnp.concatenate([cos, cos, ones], axis=1)
    sin_lo = jnp.concatenate([-sin, zeros], axis=1)
    sin_hi = jnp.concatenate([jnp.zeros((n, half), F32), sin, jnp.zeros((n, HEAD_DIM - ROT_DIM), F32)], axis=1)
    return cos_t, sin_lo, sin_hi


def _q_kernel(x_ref, g_ref, w_ref, cos_ref, slo_ref, shi_ref, q_ref):
    xn = _rms(x_ref[...], g_ref[...]).astype(BF16)
    q_ref[...] = _rope(_dot(xn, w_ref[...]), cos_ref[...], slo_ref[...], shi_ref[...])


def _q_proj(h, g, w, tables, tm):
    t, d = h.shape
    n_tab = tables[0].shape[0] // tm
    tab = pl.BlockSpec((tm, HEAD_DIM), lambda i: (i % n_tab, 0))
    row = pl.BlockSpec((tm, d), lambda i: (i, 0))
    return pl.pallas_call(
        _q_kernel,
        out_shape=jax.ShapeDtypeStruct((t, d), F32),
        grid=(t // tm,),
        in_specs=[row, _resident((1, d)), _resident(w.shape), tab, tab, tab],
        out_specs=row,
        compiler_params=_cparams("parallel"),
        name="q_proj",
    )(h, g.reshape(1, d), w, *tables)


def _kv_kernel(x_ref, g_ref, w_ref, cos_ref, slo_ref, shi_ref, k_ref, v_ref, *ksum_ref, block_sums):
    xn = _rms(x_ref[...], g_ref[...]).astype(BF16)
    k = _rope(_dot(xn, w_ref[:, :D_MODEL]), cos_ref[...], slo_ref[...], shi_ref[...])
    k_ref[...] = k
    v_ref[...] = _dot(xn, w_ref[:, D_MODEL:])
    if block_sums:
        ksum_ref[0][0] = jnp.sum(k, axis=0, keepdims=True) * (1.0 / MOBA_BLOCK)


def _kv_proj(h, g, w, tables, tm, block_sums):
    t, d = h.shape
    n_tab = tables[0].shape[0] // tm
    tab = pl.BlockSpec((tm, HEAD_DIM), lambda i: (i % n_tab, 0))
    row = pl.BlockSpec((tm, d), lambda i: (i, 0))
    out_shape = [jax.ShapeDtypeStruct((t, d), F32), jax.ShapeDtypeStruct((t, d), F32)]
    out_specs = [row, row]
    if block_sums:
        assert tm == MOBA_BLOCK
        out_shape.append(jax.ShapeDtypeStruct((t // tm, 1, d), F32))
        out_specs.append(pl.BlockSpec((1, 1, d), lambda i: (i, 0, 0)))
    return pl.pallas_call(
        functools.partial(_kv_kernel, block_sums=block_sums),
        out_shape=out_shape,
        grid=(t // tm,),
        in_specs=[row, _resident((1, d)), _resident(w.shape), tab, tab, tab],
        out_specs=out_specs,
        compiler_params=_cparams("parallel"),
        name="kv_proj",
    )(h, g.reshape(1, d), w, *tables)


def _out_kernel(*refs, gated):
    if gated:
        y_ref, z_ref, nw_ref, w_ref, h_ref, o_ref = refs
        y = _rms(y_ref[...] * _silu(z_ref[...]), nw_ref[...]).astype(BF16)
    else:
        y_ref, w_ref, h_ref, o_ref = refs
        y = y_ref[...].astype(BF16)
    o_ref[...] = h_ref[...] + _dot(y, w_ref[...])


def _out_proj(y, w, h, tm, z=None, norm_w=None):
    t, d = h.shape
    k = y.shape[1]
    row_k = pl.BlockSpec((tm, k), lambda i: (i, 0))
    row_d = pl.BlockSpec((tm, d), lambda i: (i, 0))
    gated = z is not None
    if gated:
        in_specs = [row_k, row_k, _resident((1, k)), _resident(w.shape), row_d]
        args = (y, z, norm_w.reshape(1, k), w, h)
    else:
        in_specs = [row_k, _resident(w.shape), row_d]
        args = (y, w, h)
    return pl.pallas_call(
        functools.partial(_out_kernel, gated=gated),
        out_shape=jax.ShapeDtypeStruct((t, d), F32),
        grid=(t // tm,),
        in_specs=in_specs,
        out_specs=row_d,
        compiler_params=_cparams("parallel"),
        name="out_proj",
    )(*args)


def _conv_silu(ext_ref, cur, w, b, first):
    rows = cur.shape[0]

    @pl.when(first)
    def _():
        ext_ref[0:SUBLANES, :] = jnp.zeros((SUBLANES, cur.shape[1]), F32)

    ext_ref[SUBLANES:SUBLANES + rows, :] = cur
    acc = b
    for k in range(SSM_CONV_W):
        off = SUBLANES - (SSM_CONV_W - 1) + k
        acc = acc + w[k:k + 1, :] * ext_ref[off:off + rows, :]
    ext_ref[0:SUBLANES, :] = ext_ref[rows:rows + SUBLANES, :]
    return _silu(acc)


def _split3_dot(x, w3):
    return _dot(jnp.concatenate(_split3(x), axis=1), w3)


def _ssd_kernel(x_ref, b_ref, c_ref, dt_ref, wx_ref, wb_ref, wc_ref, bx_ref, bb_ref, bc_ref,
                dtb_ref, alog_ref, dsk_ref, spread_ref, select_ref, y_ref, st_ref, xe_ref, be_ref, ce_ref):
    first = pl.program_id(2) == 0
    cs = SSD_CHUNK

    @pl.when(first)
    def _():
        st_ref[...] = jnp.zeros(st_ref.shape, F32)

    xs = _conv_silu(xe_ref, x_ref[0], wx_ref[...], bx_ref[...], first)
    bm = _conv_silu(be_ref, b_ref[0], wb_ref[...], bb_ref[...], first)
    cm = _conv_silu(ce_ref, c_ref[0], wc_ref[...], bc_ref[...], first)
    bm16, cm16 = bm.astype(BF16), cm.astype(BF16)

    row = lax.broadcasted_iota(jnp.int32, (cs, cs), 0)
    col = lax.broadcasted_iota(jnp.int32, (cs, cs), 1)
    causal = row >= col
    tril = jnp.where(causal, 1.0, 0.0).astype(BF16)
    dt = jax.nn.softplus(dt_ref[...] + dtb_ref[0])
    a = jnp.where(col < HEADS_PER_GROUP, dt * (-jnp.exp(alog_ref[0])), 0.0)
    a_hi, a_mid, a_lo = _split3(a)
    acum = _dot(tril, a_hi) + _dot(tril, a_mid) + _dot(tril, a_lo)
    acum_t = acum.T
    ac_last = acum[cs - 1:cs, :]

    spread = spread_ref[...]
    xdt = xs * _split3_dot(dt, spread)
    xdt16 = xdt.astype(BF16)
    xw = xdt * _split3_dot(jnp.exp(ac_last - acum), spread)
    ac_col = _split3_dot(acum, select_ref[...])

    cb = _dot_nt(cm16, bm16)
    st_old = st_ref[0].reshape(GROUP_X, SSM_D_STATE)
    y = _dot_nt(cm16, st_old.astype(BF16)) * _split3_dot(jnp.exp(acum), spread) + dsk_ref[0] * xs

    lane = lax.broadcasted_iota(jnp.int32, (cs, LANES), 1)
    low_half = lane < SSM_HEAD_DIM
    for pair in range(HEADS_PER_GROUP // 2):
        x_pair = xdt16[:, pair * LANES:(pair + 1) * LANES]
        y_pair = None
        for half in range(2):
            e = 2 * pair + half
            seg = ac_col[:, e * LANES:(e + 1) * LANES] - acum_t[e:e + 1, :]
            lmat = jnp.exp(jnp.where(causal, seg, NEG_INF))
            x_half = jnp.where(low_half if half == 0 else ~low_half, x_pair, jnp.zeros_like(x_pair))
            part = _dot((cb * lmat).astype(BF16), x_half)
            y_pair = part if y_pair is None else y_pair + part
        y_ref[0, :, pair * LANES:(pair + 1) * LANES] = y[:, pair * LANES:(pair + 1) * LANES] + y_pair

    states = _dot(xw.T.astype(BF16), bm16)
    for e in range(HEADS_PER_GROUP):
        lo, hi = e * SSM_HEAD_DIM, (e + 1) * SSM_HEAD_DIM
        chunk_decay = jnp.exp(ac_last[:, e:e + 1])
        st_ref[0, e] = st_ref[0, e] * chunk_decay + states[lo:hi, :]


def _group_lanes(v):
    v = v.astype(F32).reshape(SSM_GROUPS, 1, HEADS_PER_GROUP)
    return jnp.pad(v, ((0, 0), (0, 0), (0, LANES - HEADS_PER_GROUP)))


def _ssd_prompt(xbc, dt, conv_w, conv_b, dt_bias, a_log, d_skip, n_seq, seq):
    nc = seq // SSD_CHUNK
    cs = SSD_CHUNK
    xbc = xbc.reshape(n_seq, seq, SSM_CONV_DIM)
    nb = D_INNER // SSM_D_STATE
    conv_b = conv_b.reshape(1, SSM_CONV_DIM)
    par = _group_lanes
    src = jnp.arange(LANES, dtype=jnp.int32)[:, None]
    spread = (jnp.arange(GROUP_X, dtype=jnp.int32)[None, :] // SSM_HEAD_DIM == src).astype(BF16)
    select = (jnp.arange(HEADS_PER_GROUP * LANES, dtype=jnp.int32)[None, :] // LANES == src).astype(BF16)
    spread3, select3 = jnp.tile(spread, (3, 1)), jnp.tile(select, (3, 1))
    d_chan = jnp.repeat(d_skip.astype(F32), SSM_HEAD_DIM).reshape(SSM_GROUPS, 1, GROUP_X)

    def lanes(width, idx):
        return pl.BlockSpec((1, cs, width), idx)

    in_specs = [
        lanes(GROUP_X, lambda n, g, c: (n, c, g)),
        lanes(SSM_D_STATE, lambda n, g, c: (n, c, nb + g)),
        lanes(SSM_D_STATE, lambda n, g, c: (n, c, nb + SSM_GROUPS + g)),
        pl.BlockSpec((cs, LANES), lambda n, g, c: (n * nc + c, g)),
        pl.BlockSpec((SSM_CONV_W, GROUP_X), lambda n, g, c: (0, g)),
        pl.BlockSpec((SSM_CONV_W, SSM_D_STATE), lambda n, g, c: (0, nb + g)),
        pl.BlockSpec((SSM_CONV_W, SSM_D_STATE), lambda n, g, c: (0, nb + SSM_GROUPS + g)),
        pl.BlockSpec((1, GROUP_X), lambda n, g, c: (0, g)),
        pl.BlockSpec((1, SSM_D_STATE), lambda n, g, c: (0, nb + g)),
        pl.BlockSpec((1, SSM_D_STATE), lambda n, g, c: (0, nb + SSM_GROUPS + g)),
        pl.BlockSpec((1, 1, LANES), lambda n, g, c: (g, 0, 0)),
        pl.BlockSpec((1, 1, LANES), lambda n, g, c: (g, 0, 0)),
        pl.BlockSpec((1, 1, GROUP_X), lambda n, g, c: (g, 0, 0)),
        _resident(spread3.shape),
        _resident(select3.shape),
    ]
    return pl.pallas_call(
        _ssd_kernel,
        out_shape=[jax.ShapeDtypeStruct((n_seq, seq, D_INNER), F32),
                   jax.ShapeDtypeStruct((n_seq, SSM_HEADS, SSM_HEAD_DIM, SSM_D_STATE), F32)],
        grid=(n_seq, SSM_GROUPS, nc),
        in_specs=in_specs,
        out_specs=[pl.BlockSpec((1, cs, GROUP_X), lambda n, g, c: (n, c, g)),
                   pl.BlockSpec((1, HEADS_PER_GROUP, SSM_HEAD_DIM, SSM_D_STATE), lambda n, g, c: (n, g, 0, 0))],
        scratch_shapes=[pltpu.VMEM((cs + SUBLANES, GROUP_X), F32),
                        pltpu.VMEM((cs + SUBLANES, SSM_D_STATE), F32),
                        pltpu.VMEM((cs + SUBLANES, SSM_D_STATE), F32)],
        compiler_params=_cparams("parallel", "parallel", "arbitrary"),
        name="ssd_prompt",
    )(xbc, xbc, xbc, dt, conv_w, conv_w, conv_w, conv_b, conv_b, conv_b,
      par(dt_bias), par(a_log), d_chan, spread3, select3)


def _ssm_step_kernel(xbc_ref, cst_ref, dt_ref, w_ref, b_ref, dtb_ref, alog_ref, dsk_ref, st_ref,
                     y_ref, sto_ref):
    w = w_ref[...]
    acc = b_ref[...] + w[SSM_CONV_W - 1:SSM_CONV_W, :] * xbc_ref[0]
    for k in range(SSM_CONV_W - 1):
        acc = acc + w[k:k + 1, :] * cst_ref[0, k:k + 1, :]
    act = _silu(acc)
    nb = D_INNER
    row0 = lax.broadcasted_iota(jnp.int32, (SUBLANES, 1), 0) == 0
    hp = lax.Precision.HIGHEST
    for g in range(SSM_GROUPS):
        xs = act[:, g * GROUP_X:(g + 1) * GROUP_X]
        bm = act[:, nb + g * SSM_D_STATE:nb + (g + 1) * SSM_D_STATE]
        cm = act[:, nb + (SSM_GROUPS + g) * SSM_D_STATE:nb + (SSM_GROUPS + g + 1) * SSM_D_STATE]
        dt = jax.nn.softplus(dt_ref[0, :, g * LANES:(g + 1) * LANES] + dtb_ref[g])
        decay = jnp.exp(dt * (-jnp.exp(alog_ref[g])))
        xdt = jnp.concatenate(
            [xs[:, e * SSM_HEAD_DIM:(e + 1) * SSM_HEAD_DIM] * dt[:, e:e + 1] for e in range(HEADS_PER_GROUP)], axis=1)
        x8 = jnp.where(row0, xdt, 0.0)
        b8 = jnp.where(row0, bm, 0.0)
        c8 = jnp.where(row0, cm, 0.0)
        outer = lax.dot_general(x8, b8, (((0,), (0,)), ((), ())), precision=hp,
                                preferred_element_type=F32)
        new = []
        for e in range(HEADS_PER_GROUP):
            head = g * HEADS_PER_GROUP + e
            h_new = st_ref[0, head] * decay[:, e:e + 1] + outer[e * SSM_HEAD_DIM:(e + 1) * SSM_HEAD_DIM, :]
            sto_ref[0, head] = h_new
            new.append(h_new)
        h_g = jnp.concatenate(new, axis=0)
        y8 = lax.dot_general(c8, h_g, (((1,), (1,)), ((), ())), precision=hp,
                             preferred_element_type=F32)
        dsk = dsk_ref[g]
        dx = jnp.concatenate(
            [xs[:, e * SSM_HEAD_DIM:(e + 1) * SSM_HEAD_DIM] * dsk[:, e:e + 1] for e in range(HEADS_PER_GROUP)], axis=1)
        y_ref[0, :, g * GROUP_X:(g + 1) * GROUP_X] = y8[0:1, :] + dx


def _ssm_step(xbc, dt, conv_state, ssm_state, conv_w, conv_b, dt_bias, a_log, d_skip):
    n = xbc.shape[0]
    par = _group_lanes
    st_spec = pl.BlockSpec((1, SSM_HEADS, SSM_HEAD_DIM, SSM_D_STATE), lambda s: (s, 0, 0, 0))
    y, st = pl.pallas_call(
        _ssm_step_kernel,
        out_shape=[jax.ShapeDtypeStruct((n, 1, D_INNER), F32),
                   jax.ShapeDtypeStruct(ssm_state.shape, F32)],
        grid=(n,),
        in_specs=[pl.BlockSpec((1, 1, SSM_CONV_DIM), lambda s: (s, 0, 0)),
                  pl.BlockSpec((1, SSM_CONV_W - 1, SSM_CONV_DIM), lambda s: (s, 0, 0)),
                  pl.BlockSpec((1, 1, SSM_GROUPS * LANES), lambda s: (s, 0, 0)),
                  _resident((SSM_CONV_W, SSM_CONV_DIM)),
                  _resident((1, SSM_CONV_DIM)),
                  _resident((SSM_GROUPS, 1, LANES)),
                  _resident((SSM_GROUPS, 1, LANES)),
                  _resident((SSM_GROUPS, 1, LANES)),
                  st_spec],
        out_specs=[pl.BlockSpec((1, 1, D_INNER), lambda s: (s, 0, 0)), st_spec],
        compiler_params=_cparams("parallel"),
        name="ssm_step",
    )(xbc.reshape(n, 1, SSM_CONV_DIM), conv_state, dt.reshape(n, 1, SSM_GROUPS * LANES), conv_w,
      conv_b.reshape(1, SSM_CONV_DIM), par(dt_bias), par(a_log), par(d_skip), ssm_state)
    return y.reshape(n, D_INNER), st


MOBA_HEADS_PER_STEP = 2


def _moba_kernel(q_ref, k_ref, vt_ref, mean_ref, o_ref, bias_ref):
    i = pl.program_id(2)
    blk = MOBA_BLOCK
    n_blk = mean_ref.shape[1]
    n_heads = q_ref.shape[2] // HEAD_DIM
    sub = lax.broadcasted_iota(jnp.int32, (n_blk, blk), 0)
    own_row = jnp.full((1, blk), i, jnp.int32)
    key = lax.broadcasted_iota(jnp.int32, (blk, blk), 0)
    qry = lax.broadcasted_iota(jnp.int32, (blk, blk), 1)
    own = pl.multiple_of(i * blk, blk)

    qs_t, carry = [], []
    for hd in range(n_heads):
        lanes = slice(hd * HEAD_DIM, (hd + 1) * HEAD_DIM)
        q_t = q_ref[0, :, lanes].T
        gate = jnp.dot(mean_ref[0, :, lanes], q_t, precision=lax.Precision.HIGHEST,
                       preferred_element_type=F32)
        gate = jnp.where(sub < i, gate, NEG_INF)
        for j in range(n_blk):
            g_j = gate[j:j + 1, :]
            beats = (gate > g_j) | ((gate == g_j) & (sub < j))
            rank = jnp.sum(jnp.where(beats, 1.0, 0.0), axis=0, keepdims=True)
            chosen = (rank < MOBA_TOPK) & (own_row > j)
            bias_ref[hd, j] = jnp.where(chosen, 0.0, NEG_INF)

        q_t = (q_t * (HEAD_DIM ** -0.5)).astype(BF16)
        s = _dot(k_ref[0, pl.ds(own, blk), lanes], q_t)
        s = jnp.where(key <= qry, s, NEG_INF)
        m = jnp.max(s, axis=0, keepdims=True)
        p = jnp.exp(s - m)
        l = jnp.sum(p, axis=0, keepdims=True)
        acc = _dot(vt_ref[0, i, lanes, :], p.astype(BF16))
        qs_t.append(q_t)
        carry.append((m, l, acc))

    def body(jj, carry):
        j0 = 2 * jj
        j1 = j0 + 1
        off = pl.multiple_of(j0 * blk, 2 * blk)
        out = []
        for hd in range(n_heads):
            lanes = slice(hd * HEAD_DIM, (hd + 1) * HEAD_DIM)
            m, l, acc = carry[hd]
            s0 = _dot(k_ref[0, pl.ds(off, blk), lanes], qs_t[hd]) + bias_ref[hd, j0]
            s1 = _dot(k_ref[0, pl.ds(off + blk, blk), lanes], qs_t[hd]) + bias_ref[hd, j1]
            m_new = jnp.maximum(m, jnp.max(jnp.maximum(s0, s1), axis=0, keepdims=True))
            alpha = jnp.exp(m - m_new)
            p0 = jnp.exp(s0 - m_new)
            p1 = jnp.exp(s1 - m_new)
            l = alpha * l + jnp.sum(p0 + p1, axis=0, keepdims=True)
            acc = (alpha * acc + _dot(vt_ref[0, j0, lanes, :], p0.astype(BF16))
                   + _dot(vt_ref[0, j1, lanes, :], p1.astype(BF16)))
            out.append((m_new, l, acc))
        return tuple(out)

    carry = lax.fori_loop(0, (i + 1) // 2, body, tuple(carry))
    for hd in range(n_heads):
        m, l, acc = carry[hd]
        o_ref[0, :, hd * HEAD_DIM:(hd + 1) * HEAD_DIM] = (acc / l).T.astype(o_ref.dtype)


def _moba_prompt(q, k16, v16, kmean, n_seq, seq):
    n_blk = seq // MOBA_BLOCK
    width = MOBA_HEADS_PER_STEP * HEAD_DIM
    v_t = v16.reshape(n_seq, n_blk, MOBA_BLOCK, D_MODEL).transpose(0, 1, 3, 2)
    tile = pl.BlockSpec((1, MOBA_BLOCK, width), lambda n, h, i: (n, i, h))
    return pl.pallas_call(
        _moba_kernel,
        out_shape=jax.ShapeDtypeStruct((n_seq, seq, D_MODEL), BF16),
        grid=(n_seq, N_HEADS // MOBA_HEADS_PER_STEP, n_blk),
        in_specs=[tile,
                  pl.BlockSpec((1, seq, width), lambda n, h, i: (n, 0, h)),
                  pl.BlockSpec((1, n_blk, width, MOBA_BLOCK), lambda n, h, i: (n, 0, h, 0)),
                  pl.BlockSpec((1, n_blk, width), lambda n, h, i: (n, 0, h))],
        out_specs=tile,
        scratch_shapes=[pltpu.VMEM((MOBA_HEADS_PER_STEP, n_blk, 1, MOBA_BLOCK), F32)],
        compiler_params=_cparams("parallel", "parallel", "arbitrary"),
        name="moba_prompt",
    )(q, k16, v_t, kmean)


MEAN_BLOCKS_PER_STEP = 4


def _page_mean_kernel(pt_ref, *refs):
    del pt_ref
    o_ref = refs[-1]
    for b in range(MEAN_BLOCKS_PER_STEP):
        total = jnp.sum(refs[PAGES_PER_BLOCK * b][0], axis=0)
        for t in range(1, PAGES_PER_BLOCK):
            total = total + jnp.sum(refs[PAGES_PER_BLOCK * b + t][0], axis=0)
        o_ref[0, b] = total * (1.0 / MOBA_BLOCK)


def _page_means(cache_k, page_table):
    n_seq, n_pages = page_table.shape
    n_blk = n_pages // PAGES_PER_BLOCK
    per_step = MEAN_BLOCKS_PER_STEP * PAGES_PER_BLOCK
    page = lambda t: pl.BlockSpec(
        (1, PAGE_SIZE, N_HEADS, HEAD_DIM), lambda s, b, pt: (pt[s * n_pages + per_step * b + t], 0, 0, 0))
    out = pl.pallas_call(
        _page_mean_kernel,
        out_shape=jax.ShapeDtypeStruct((n_seq, n_blk, N_HEADS, HEAD_DIM), F32),
        grid_spec=pltpu.PrefetchScalarGridSpec(
            num_scalar_prefetch=1,
            grid=(n_seq, n_blk // MEAN_BLOCKS_PER_STEP),
            in_specs=[page(t) for t in range(per_step)],
            out_specs=pl.BlockSpec((1, MEAN_BLOCKS_PER_STEP, N_HEADS, HEAD_DIM), lambda s, b, pt: (s, b, 0, 0)),
        ),
        compiler_params=_cparams("parallel", "parallel"),
        name="page_means",
    )(page_table.reshape(-1), *([cache_k] * per_step))
    return out.reshape(n_seq, n_blk, D_MODEL)


def _select_kernel(q_ref, mean_ref, idx_ref):
    n_blk = mean_ref.shape[1]
    prod = mean_ref[0] * q_ref[0]
    lane = lax.broadcasted_iota(jnp.int32, (n_blk, LANES), 1)
    sub = lax.broadcasted_iota(jnp.int32, (n_blk, LANES), 0)
    gate = jnp.zeros((n_blk, LANES), F32)
    for hd in range(N_HEADS):
        g_h = jnp.sum(prod[:, hd * HEAD_DIM:(hd + 1) * HEAD_DIM], axis=1, keepdims=True)
        gate = jnp.where(lane == hd, g_h, gate)
    out_sub = lax.broadcasted_iota(jnp.int32, (SUBLANES, LANES), 0)
    out = jnp.zeros((SUBLANES, LANES), jnp.int32)
    for j in range(n_blk):
        g_j = gate[j:j + 1, :]
        beats = (gate > g_j) | ((gate == g_j) & (sub < j))
        rank = jnp.sum(jnp.where(beats, 1.0, 0.0), axis=0, keepdims=True)
        for r in range(MOBA_TOPK):
            out = jnp.where((out_sub == r) & (rank == float(r)), j, out)
    idx_ref[0] = out


def _select_blocks(q, means):
    n_seq, n_blk, _ = means.shape
    idx = pl.pallas_call(
        _select_kernel,
        out_shape=jax.ShapeDtypeStruct((n_seq, SUBLANES, LANES), jnp.int32),
        grid=(n_seq,),
        in_specs=[pl.BlockSpec((1, 1, D_MODEL), lambda s: (s, 0, 0)),
                  pl.BlockSpec((1, n_blk, D_MODEL), lambda s: (s, 0, 0))],
        out_specs=pl.BlockSpec((1, SUBLANES, LANES), lambda s: (s, 0, 0)),
        compiler_params=_cparams("parallel"),
        name="select_blocks",
    )(q.reshape(n_seq, 1, D_MODEL), means)
    return idx[:, :MOBA_TOPK, :N_HEADS]


N_SEL_PAGES = MOBA_TOPK * PAGES_PER_BLOCK


def _decode_attn_kernel(pt_ref, idx_ref, q_ref, kn_ref, vn_ref, k_hbm, v_hbm, o_ref, kbuf, vbuf, sems,
                        *, n_pages):
    s_i = pl.program_id(0)

    def copies(hd, slot):
        r, t = divmod(slot, PAGES_PER_BLOCK)
        blk = idx_ref[(s_i * MOBA_TOPK + r) * N_HEADS + hd]
        pg = pt_ref[s_i * n_pages + PAGES_PER_BLOCK * blk + t]
        return (pltpu.make_async_copy(k_hbm.at[pg, :, hd, :], kbuf.at[hd, slot], sems.at[0, hd, slot]),
                pltpu.make_async_copy(v_hbm.at[pg, :, hd, :], vbuf.at[hd, slot], sems.at[1, hd, slot]))

    for hd in range(N_HEADS):
        for slot in range(N_SEL_PAGES):
            for cp in copies(hd, slot):
                cp.start()

    for hd in range(N_HEADS):
        lanes = slice(hd * HEAD_DIM, (hd + 1) * HEAD_DIM)
        for slot in range(N_SEL_PAGES):
            for cp in copies(hd, slot):
                cp.wait()
        q = q_ref[0, :, lanes] * (HEAD_DIM ** -0.5)
        s_own = jnp.sum(q * kn_ref[0, :, lanes], axis=1, keepdims=True)
        scores = [jnp.sum(kbuf[hd, slot] * q, axis=1, keepdims=True) for slot in range(N_SEL_PAGES)]
        m = s_own
        for s in scores:
            m = jnp.maximum(m, jnp.max(s, axis=0, keepdims=True))
        p_own = jnp.exp(s_own - m)
        l = p_own
        acc = p_own * vn_ref[0, :, lanes]
        for slot, s in enumerate(scores):
            p = jnp.exp(s - m)
            l = l + jnp.sum(p, axis=0, keepdims=True)
            acc = acc + jnp.sum(p * vbuf[hd, slot], axis=0, keepdims=True)
        o_ref[0, :, lanes] = acc / l


def _decode_attn(q, k_new, v_new, cache_k, cache_v, page_table, idx):
    n_seq, n_pages = page_table.shape
    vec = pl.BlockSpec((1, 1, D_MODEL), lambda s, pt, sel: (s, 0, 0))
    hbm = pl.BlockSpec(memory_space=pl.ANY)
    r3 = lambda a: a.reshape(n_seq, 1, D_MODEL)
    slabs = (N_HEADS, N_SEL_PAGES, PAGE_SIZE, HEAD_DIM)
    out = pl.pallas_call(
        functools.partial(_decode_attn_kernel, n_pages=n_pages),
        out_shape=jax.ShapeDtypeStruct((n_seq, 1, D_MODEL), F32),
        grid_spec=pltpu.PrefetchScalarGridSpec(
            num_scalar_prefetch=2,
            grid=(n_seq,),
            in_specs=[vec, vec, vec, hbm, hbm],
            out_specs=vec,
            scratch_shapes=[pltpu.VMEM(slabs, F32), pltpu.VMEM(slabs, F32),
                            pltpu.SemaphoreType.DMA((2, N_HEADS, N_SEL_PAGES))],
        ),
        compiler_params=_cparams("arbitrary"),
        name="decode_attn",
    )(page_table.reshape(-1), idx.reshape(-1), r3(q), r3(k_new), r3(v_new), cache_k, cache_v)
    return out.reshape(n_seq, D_MODEL)


def _in_proj_weight(w_in):
    w_z = w_in[:, :D_INNER]
    w_xbc = w_in[:, D_INNER:D_INNER + SSM_CONV_DIM]
    w_dt = w_in[:, D_INNER + SSM_CONV_DIM:].reshape(D_MODEL, SSM_GROUPS, HEADS_PER_GROUP)
    w_dt = jnp.pad(w_dt, ((0, 0), (0, 0), (0, LANES - HEADS_PER_GROUP))).reshape(D_MODEL, SSM_GROUPS * LANES)
    return jnp.concatenate([w_z, w_xbc, w_dt], axis=1).astype(BF16)


IN_WIDTHS = (D_INNER, SSM_CONV_DIM, SSM_GROUPS * LANES)


def kernel(x_prompt, x_sample, state_conv, state_ssm, cache_k, cache_v, page_table, norm_ffn_a, w_ffn_a_up, w_ffn_a_down, norm_mix, norm_ffn_b, w_ffn_b_up, w_ffn_b_down, ssm_w_in, ssm_conv_w, ssm_conv_b, ssm_dt_bias, ssm_a_log, ssm_d, ssm_norm, ssm_w_out, norm_kv, w_kv, w_q, w_o, norm_final):
    n_p, seq, d = x_prompt.shape
    n_s = x_sample.shape[0]
    past_len = page_table.shape[1] * PAGE_SIZE
    tm_p = 512
    tm_s = n_s

    hp = x_prompt.reshape(n_p * seq, d)
    hs = x_sample.reshape(n_s, d)
    tab_p = _rope_tables(jnp.arange(seq, dtype=jnp.int32))
    tab_s = _rope_tables(jnp.full((n_s,), past_len, jnp.int32))

    conv_p, ssm_p, conv_s, ssm_s = [], [], [], []
    k_p = v_p = k_s = v_s = None
    for layer in range(DEPTH):
        if layer == N_A_LAYERS:
            w_kv16 = w_kv.astype(BF16)
            k_p, v_p, kmean_p = _kv_proj(hp, norm_kv, w_kv16, tab_p, MOBA_BLOCK, True)
            k_s, v_s = _kv_proj(hs, norm_kv, w_kv16, tab_s, tm_s, False)
            k16 = k_p.astype(BF16).reshape(n_p, seq, d)
            v16 = v_p.astype(BF16).reshape(n_p, seq, d)
            kmean_p = kmean_p.reshape(n_p, seq // MOBA_BLOCK, d)
            kmean_s = _page_means(cache_k, page_table)

        w_up, w_dn = w_ffn_a_up[layer].astype(BF16), w_ffn_a_down[layer].astype(BF16)
        hp = _ffn(hp, norm_ffn_a[layer], w_up, w_dn, tm_p)
        hs = _ffn(hs, norm_ffn_a[layer], w_up, w_dn, tm_s)

        if layer < N_A_LAYERS:
            w_in = _in_proj_weight(ssm_w_in[layer])
            w_out = ssm_w_out[layer].astype(BF16)
            ssm_args = (ssm_conv_w[layer], ssm_conv_b[layer], ssm_dt_bias[layer], ssm_a_log[layer], ssm_d[layer])
            z, xbc, dt = _proj(hp, norm_mix[layer], w_in, IN_WIDTHS, tm_p)
            y, st = _ssd_prompt(xbc, dt, *ssm_args, n_p, seq)
            conv_p.append(xbc.reshape(n_p, seq, SSM_CONV_DIM)[:, seq - (SSM_CONV_W - 1):])
            ssm_p.append(st)
            hp = _out_proj(y.reshape(n_p * seq, D_INNER), w_out, hp, tm_p, z=z, norm_w=ssm_norm[layer])

            z, xbc, dt = _proj(hs, norm_mix[layer], w_in, IN_WIDTHS, tm_s)
            y, st = _ssm_step(xbc, dt, state_conv[layer], state_ssm[layer], *ssm_args)
            conv_s.append(jnp.concatenate([state_conv[layer][:, 1:], xbc[:, None, :]], axis=1))
            ssm_s.append(st)
            hs = _out_proj(y, w_out, hs, tm_s, z=z, norm_w=ssm_norm[layer])
        else:
            j = layer - N_A_LAYERS
            w_q16, w_o16 = w_q[j].astype(BF16), w_o[j].astype(BF16)
            q = _q_proj(hp, norm_mix[layer], w_q16, tab_p, tm_p)
            o = _moba_prompt(q.reshape(n_p, seq, d), k16, v16, kmean_p, n_p, seq)
            hp = _out_proj(o.reshape(n_p * seq, d), w_o16, hp, tm_p)

            q = _q_proj(hs, norm_mix[layer], w_q16, tab_s, tm_s)
            idx = _select_blocks(q, kmean_s)
            o = _decode_attn(q, k_s, v_s, cache_k, cache_v, page_table, idx)
            hs = _out_proj(o, w_o16, hs, tm_s)

        w_up, w_dn = w_ffn_b_up[layer].astype(BF16), w_ffn_b_down[layer].astype(BF16)
        final_g = norm_final if layer == DEPTH - 1 else None
        hp = _ffn(hp, norm_ffn_b[layer], w_up, w_dn, tm_p, final_g)
        hs = _ffn(hs, norm_ffn_b[layer], w_up, w_dn, tm_s, final_g)

    heads = (N_HEADS, HEAD_DIM)
    return (hp.reshape(n_p, seq, d), hs.reshape(n_s, 1, d),
            jnp.stack(conv_p), jnp.stack(ssm_p),
            k_p.reshape(n_p, seq, *heads), v_p.reshape(n_p, seq, *heads),
            jnp.stack(conv_s), jnp.stack(ssm_s),
            k_s.reshape(n_s, 1, *heads), v_s.reshape(n_s, 1, *heads))
```

```python
import functools
import math

import jax
import jax.numpy as jnp
from jax import lax
from jax.experimental import pallas as pl
from jax.experimental.pallas import tpu as pltpu

F32 = jnp.float32
BF16 = jnp.bfloat16

D_MODEL = 1024
DEPTH = 4
N_A_LAYERS = 2
NORM_EPS = 1e-5
D_FF = 2816
D_INNER = 2048
SSM_HEAD_DIM = 64
SSM_HEADS = 32
SSM_GROUPS = 4
HEADS_PER_GROUP = 8
SSM_D_STATE = 128
SSM_CONV_W = 4
SSM_CONV_DIM = D_INNER + 2 * SSM_GROUPS * SSM_D_STATE
SSD_CHUNK = 128
GROUP_X = HEADS_PER_GROUP * SSM_HEAD_DIM
N_HEADS = 8
HEAD_DIM = 128
ROT_DIM = 32
ROPE_THETA = 500000.0
MOBA_BLOCK = 256
MOBA_TOPK = 3
PAGE_SIZE = 128
PAGES_PER_BLOCK = MOBA_BLOCK // PAGE_SIZE

LANES = 128
SUBLANES = 8
VMEM_LIMIT_BYTES = 56 * 1024 * 1024
FF_CHUNK = 256
NEG_INF = float("-inf")


def _cparams(*sem):
    return pltpu.CompilerParams(dimension_semantics=sem, vmem_limit_bytes=VMEM_LIMIT_BYTES)


def _resident(shape):
    nd = len(shape)
    return pl.BlockSpec(shape, lambda *_: (0,) * nd, pipeline_mode=pl.Buffered(1))


def _rms(x, g):
    return x * lax.rsqrt(jnp.mean(x * x, axis=-1, keepdims=True) + NORM_EPS) * g


def _silu(x):
    return x * jax.nn.sigmoid(x)


def _split3(x):
    hi = x.astype(BF16)
    r1 = x - hi.astype(F32)
    mid = r1.astype(BF16)
    lo = (r1 - mid.astype(F32)).astype(BF16)
    return hi, mid, lo


def _dot(a, b):
    return jnp.dot(a, b, preferred_element_type=F32)


def _dot_nt(a, b):
    return lax.dot_general(a, b, (((1,), (1,)), ((), ())), preferred_element_type=F32)


def _ffn_kernel(*refs, final_norm):
    if final_norm:
        x_ref, g_ref, wup_ref, wdn_ref, gf_ref, o_ref = refs
    else:
        x_ref, g_ref, wup_ref, wdn_ref, o_ref = refs
    x = x_ref[...]
    xn = _rms(x, g_ref[...]).astype(BF16)
    acc = jnp.zeros_like(x)
    for c in range(D_FF // FF_CHUNK):
        lo, hi = c * FF_CHUNK, (c + 1) * FF_CHUNK
        gate = _dot(xn, wup_ref[:, lo:hi])
        up = _dot(xn, wup_ref[:, D_FF + lo:D_FF + hi])
        act = (_silu(gate) * up).astype(BF16)
        acc = acc + _dot(act, wdn_ref[lo:hi, :])
    h = x + 0.5 * acc
    if final_norm:
        h = _rms(h, gf_ref[...])
    o_ref[...] = h


def _ffn(h, g, w_up, w_dn, tm, final_g=None):
    t, d = h.shape
    row = pl.BlockSpec((tm, d), lambda i: (i, 0))
    in_specs = [row, _resident((1, d)), _resident(w_up.shape), _resident(w_dn.shape)]
    args = [h, g.reshape(1, d), w_up, w_dn]
    if final_g is not None:
        in_specs.append(_resident((1, d)))
        args.append(final_g.reshape(1, d))
    return pl.pallas_call(
        functools.partial(_ffn_kernel, final_norm=final_g is not None),
        out_shape=jax.ShapeDtypeStruct((t, d), F32),
        grid=(t // tm,),
        in_specs=in_specs,
        out_specs=row,
        compiler_params=_cparams("parallel"),
        name="ffn",
    )(*args)


PROJ_CHUNK = 512


def _store_proj(xn, w_ref, o_ref, col0, width):
    for c0 in range(0, width, PROJ_CHUNK):
        c1 = min(c0 + PROJ_CHUNK, width)
        o_ref[:, c0:c1] = _dot(xn, w_ref[:, col0 + c0:col0 + c1]).astype(o_ref.dtype)


def _proj_kernel(x_ref, g_ref, w_ref, *o_refs):
    xn = _rms(x_ref[...], g_ref[...]).astype(BF16)
    col = 0
    for o_ref in o_refs:
        width = o_ref.shape[1]
        _store_proj(xn, w_ref, o_ref, col, width)
        col += width


def _proj(h, g, w, widths, tm):
    t, d = h.shape
    return pl.pallas_call(
        _proj_kernel,
        out_shape=[jax.ShapeDtypeStruct((t, n), F32) for n in widths],
        grid=(t // tm,),
        in_specs=[pl.BlockSpec((tm, d), lambda i: (i, 0)), _resident((1, d)), _resident(w.shape)],
        out_specs=[pl.BlockSpec((tm, n), lambda i: (i, 0)) for n in widths],
        compiler_params=_cparams("parallel"),
        name="proj",
    )(h, g.reshape(1, d), w)


def _rope(y, cos, sin_lo, sin_hi):
    outs = []
    for hd in range(y.shape[1] // HEAD_DIM):
        xh = y[:, hd * HEAD_DIM:(hd + 1) * HEAD_DIM]
        up = pltpu.roll(xh, HEAD_DIM - ROT_DIM // 2, axis=1)
        dn = pltpu.roll(xh, ROT_DIM // 2, axis=1)
        outs.append(xh * cos + up * sin_lo + dn * sin_hi)
    return jnp.concatenate(outs, axis=1)


def _rope_tables(pos):
    half = ROT_DIM // 2
    inv = ROPE_THETA ** (-2.0 * jnp.arange(half, dtype=F32) / ROT_DIM)
    ang = pos.astype(F32)[:, None] * inv[None, :]
    cos, sin = jnp.cos(ang), jnp.sin(ang)
    n = pos.shape[0]
    ones = jnp.ones((n, HEAD_DIM - ROT_DIM), F32)
    zeros = jnp.zeros((n, HEAD_DIM - half), F32)
    cos_t = jnp.concatenate([cos, cos, ones], axis=1)
    sin_lo = jnp.concatenate([-sin, zeros], axis=1)
    sin_hi = jnp.concatenate([jnp.zeros((n, half), F32), sin, jnp.zeros((n, HEAD_DIM - ROT_DIM), F32)], axis=1)
    return cos_t, sin_lo, sin_hi


def _q_kernel(x_ref, g_ref, w_ref, cos_ref, slo_ref, shi_ref, q_ref):
    xn = _rms(x_ref[...], g_ref[...]).astype(BF16)
    q_ref[...] = _rope(_dot(xn, w_ref[...]), cos_ref[...], slo_ref[...], shi_ref[...])


def _q_proj(h, g, w, tables, tm):
    t, d = h.shape
    n_tab = tables[0].shape[0] // tm
    tab = pl.BlockSpec((tm, HEAD_DIM), lambda i: (i % n_tab, 0))
    row = pl.BlockSpec((tm, d), lambda i: (i, 0))
    return pl.pallas_call(
        _q_kernel,
        out_shape=jax.ShapeDtypeStruct((t, d), F32),
        grid=(t // tm,),
        in_specs=[row, _resident((1, d)), _resident(w.shape), tab, tab, tab],
        out_specs=row,
        compiler_params=_cparams("parallel"),
        name="q_proj",
    )(h, g.reshape(1, d), w, *tables)


def _kv_kernel(x_ref, g_ref, w_ref, cos_ref, slo_ref, shi_ref, k_ref, v_ref, *block_refs, block_sums):
    xn = _rms(x_ref[...], g_ref[...]).astype(BF16)
    k = _rope(_dot(xn, w_ref[:, :D_MODEL]), cos_ref[...], slo_ref[...], shi_ref[...])
    v = _dot(xn, w_ref[:, D_MODEL:])
    k_ref[...] = k
    v_ref[...] = v
    if block_sums:
        ksum_ref, k16_ref, vt16_ref = block_refs
        ksum_ref[0] = jnp.sum(k, axis=0, keepdims=True) * (1.0 / MOBA_BLOCK)
        k16_ref[...] = k.astype(BF16)
        vt16_ref[0] = v.T.astype(BF16)


def _kv_proj(h, g, w, tables, tm, block_sums):
    t, d = h.shape
    n_tab = tables[0].shape[0] // tm
    tab = pl.BlockSpec((tm, HEAD_DIM), lambda i: (i % n_tab, 0))
    row = pl.BlockSpec((tm, d), lambda i: (i, 0))
    out_shape = [jax.ShapeDtypeStruct((t, d), F32), jax.ShapeDtypeStruct((t, d), F32)]
    out_specs = [row, row]
    if block_sums:
        assert tm == MOBA_BLOCK
        out_shape += [jax.ShapeDtypeStruct((t // tm, 1, d), F32), jax.ShapeDtypeStruct((t, d), BF16),
                      jax.ShapeDtypeStruct((t // tm, d, tm), BF16)]
        out_specs += [pl.BlockSpec((1, 1, d), lambda i: (i, 0, 0)), row,
                      pl.BlockSpec((1, d, tm), lambda i: (i, 0, 0))]
    return pl.pallas_call(
        functools.partial(_kv_kernel, block_sums=block_sums),
        out_shape=out_shape,
        grid=(t // tm,),
        in_specs=[row, _resident((1, d)), _resident(w.shape), tab, tab, tab],
        out_specs=out_specs,
        compiler_params=_cparams("parallel"),
        name="kv_proj",
    )(h, g.reshape(1, d), w, *tables)


def _out_kernel(*refs, gated):
    if gated:
        y_ref, z_ref, nw_ref, w_ref, h_ref, o_ref = refs
        y = _rms(y_ref[...].astype(F32) * _silu(z_ref[...].astype(F32)), nw_ref[...]).astype(BF16)
    else:
        y_ref, w_ref, h_ref, o_ref = refs
        y = y_ref[...].astype(BF16)
    o_ref[...] = h_ref[...] + _dot(y, w_ref[...])


def _out_proj(y, w, h, tm, z=None, norm_w=None):
    t, d = h.shape
    k = y.shape[1]
    row_k = pl.BlockSpec((tm, k), lambda i: (i, 0))
    row_d = pl.BlockSpec((tm, d), lambda i: (i, 0))
    gated = z is not None
    if gated:
        in_specs = [row_k, row_k, _resident((1, k)), _resident(w.shape), row_d]
        args = (y, z, norm_w.reshape(1, k), w, h)
    else:
        in_specs = [row_k, _resident(w.shape), row_d]
        args = (y, w, h)
    return pl.pallas_call(
        functools.partial(_out_kernel, gated=gated),
        out_shape=jax.ShapeDtypeStruct((t, d), F32),
        grid=(t // tm,),
        in_specs=in_specs,
        out_specs=row_d,
        compiler_params=_cparams("parallel"),
        name="out_proj",
    )(*args)


def _in_proj_kernel(x_ref, g_ref, w_ref, wdt_ref, cw_ref, cb_ref, z_ref, xbc_ref, dt_ref, tail_ref, ext_ref,
                    *, tiles_per_seq):
    tm = x_ref.shape[0]
    xn = _rms(x_ref[...], g_ref[...]).astype(BF16)

    @pl.when(pl.program_id(0) % tiles_per_seq == 0)
    def _():
        ext_ref[...] = jnp.zeros((SUBLANES, SSM_CONV_DIM), F32)

    first_rows = lax.broadcasted_iota(jnp.int32, (SUBLANES, PROJ_CHUNK), 0)

    def xbc_chunk(c0):
        return _dot(xn, w_ref[:, D_INNER + c0:D_INNER + c0 + PROJ_CHUNK])

    other = [(z_ref, c0, w_ref) for c0 in range(0, D_INNER, PROJ_CHUNK)]
    other.append((dt_ref, 0, wdt_ref))
    pending = xbc_chunk(0)
    for c0 in range(0, SSM_CONV_DIM, PROJ_CHUNK):
        cols = slice(c0, c0 + PROJ_CHUNK)
        raw = pending
        if c0 + PROJ_CHUNK < SSM_CONV_DIM:
            pending = xbc_chunk(c0 + PROJ_CHUNK)
        if other:
            o_ref, dst, src_ref = other.pop(0)
            o_ref[:, dst:dst + PROJ_CHUNK] = _dot(xn, src_ref[:, dst:dst + PROJ_CHUNK]).astype(o_ref.dtype)
        prev = ext_ref[:, cols]
        acc = cb_ref[:, cols] + cw_ref[SSM_CONV_W - 1:SSM_CONV_W, cols] * raw
        for back in range(1, SSM_CONV_W):
            cur = pltpu.roll(raw, back, axis=0)
            head = jnp.where(first_rows < back, pltpu.roll(prev, back, axis=0), cur[:SUBLANES])
            shifted = jnp.concatenate([head, cur[SUBLANES:]], axis=0)
            acc = acc + cw_ref[SSM_CONV_W - 1 - back:SSM_CONV_W - back, cols] * shifted
        xbc_ref[:, cols] = _silu(acc).astype(xbc_ref.dtype)
        tail = raw[tm - SUBLANES:]
        tail_ref[0, :, cols] = tail
        ext_ref[:, cols] = tail


def _in_proj(h, g, w, w_dt, conv_w, conv_b, tm, seq):
    t, d = h.shape
    tiles_per_seq = seq // tm
    row = lambda n: pl.BlockSpec((tm, n), lambda i: (i, 0))
    return pl.pallas_call(
        functools.partial(_in_proj_kernel, tiles_per_seq=tiles_per_seq),
        out_shape=[jax.ShapeDtypeStruct((t, D_INNER), BF16),
                   jax.ShapeDtypeStruct((t, SSM_CONV_DIM), BF16),
                   jax.ShapeDtypeStruct((t, SSM_GROUPS * LANES), F32),
                   jax.ShapeDtypeStruct((t // seq, SUBLANES, SSM_CONV_DIM), F32)],
        grid=(t // tm,),
        in_specs=[row(d), _resident((1, d)), _resident(w.shape), _resident(w_dt.shape),
                  _resident(conv_w.shape), _resident((1, SSM_CONV_DIM))],
        out_specs=[row(D_INNER), row(SSM_CONV_DIM), row(SSM_GROUPS * LANES),
                   pl.BlockSpec((1, SUBLANES, SSM_CONV_DIM), lambda i: (i // tiles_per_seq, 0, 0))],
        scratch_shapes=[pltpu.VMEM((SUBLANES, SSM_CONV_DIM), F32)],
        compiler_params=_cparams("arbitrary"),
        name="in_proj",
    )(h, g.reshape(1, d), w, w_dt, conv_w, conv_b.reshape(1, SSM_CONV_DIM))


def _split3_dot(x, w3):
    return _dot(jnp.concatenate(_split3(x), axis=1), w3)


def _ssd_kernel(x_ref, b_ref, c_ref, dt_ref, dtb_ref, alog_ref, dsk_ref, spread_ref, select_ref,
                y_ref, st_ref):
    cs = SSD_CHUNK

    @pl.when(pl.program_id(2) == 0)
    def _():
        st_ref[...] = jnp.zeros(st_ref.shape, F32)

    xs = x_ref[0].astype(F32)
    bm16, cm16 = b_ref[0], c_ref[0]

    row = lax.broadcasted_iota(jnp.int32, (cs, cs), 0)
    col = lax.broadcasted_iota(jnp.int32, (cs, cs), 1)
    causal = row >= col
    tril = jnp.where(causal, 1.0, 0.0).astype(BF16)
    dt = jax.nn.softplus(dt_ref[...] + dtb_ref[0])
    a = jnp.where(col < HEADS_PER_GROUP, dt * (-jnp.exp(alog_ref[0])), 0.0)
    a_hi, a_mid, a_lo = _split3(a)
    acum = _dot(tril, a_hi) + _dot(tril, a_mid) + _dot(tril, a_lo)
    acum_t = acum.T
    ac_last = acum[cs - 1:cs, :]

    spread = spread_ref[...]
    xdt = xs * _split3_dot(dt, spread)
    xdt16 = xdt.astype(BF16)
    xw = xdt * _split3_dot(jnp.exp(ac_last - acum), spread)
    ac_col = _split3_dot(acum, select_ref[...])

    cb = _dot_nt(cm16, bm16)
    st_old = st_ref[0].reshape(GROUP_X, SSM_D_STATE)
    y = _dot_nt(cm16, st_old.astype(BF16)) * _split3_dot(jnp.exp(acum), spread) + dsk_ref[0] * xs

    lane = lax.broadcasted_iota(jnp.int32, (cs, LANES), 1)
    low_half = lane < SSM_HEAD_DIM
    for pair in range(HEADS_PER_GROUP // 2):
        x_pair = xdt16[:, pair * LANES:(pair + 1) * LANES]
        y_pair = None
        for half in range(2):
            e = 2 * pair + half
            seg = ac_col[:, e * LANES:(e + 1) * LANES] - acum_t[e:e + 1, :]
            lmat = jnp.exp(jnp.where(causal, seg, NEG_INF))
            x_half = jnp.where(low_half if half == 0 else ~low_half, x_pair, jnp.zeros_like(x_pair))
            part = _dot((cb * lmat).astype(BF16), x_half)
            y_pair = part if y_pair is None else y_pair + part
        y_ref[0, :, pair * LANES:(pair + 1) * LANES] = (
            y[:, pair * LANES:(pair + 1) * LANES] + y_pair).astype(y_ref.dtype)

    states = _dot(xw.T.astype(BF16), bm16)
    for e in range(HEADS_PER_GROUP):
        lo, hi = e * SSM_HEAD_DIM, (e + 1) * SSM_HEAD_DIM
        chunk_decay = jnp.exp(ac_last[:, e:e + 1])
        st_ref[0, e] = st_ref[0, e] * chunk_decay + states[lo:hi, :]


def _group_lanes(v):
    v = v.astype(F32).reshape(SSM_GROUPS, 1, HEADS_PER_GROUP)
    return jnp.pad(v, ((0, 0), (0, 0), (0, LANES - HEADS_PER_GROUP)))


def _ssd_prompt(xbc, dt, dt_bias, a_log, d_skip, n_seq, seq):
    nc = seq // SSD_CHUNK
    cs = SSD_CHUNK
    xbc = xbc.reshape(n_seq, seq, SSM_CONV_DIM)
    nb = D_INNER // SSM_D_STATE
    par = _group_lanes
    src = jnp.arange(LANES, dtype=jnp.int32)[:, None]
    spread = (jnp.arange(GROUP_X, dtype=jnp.int32)[None, :] // SSM_HEAD_DIM == src).astype(BF16)
    select = (jnp.arange(HEADS_PER_GROUP * LANES, dtype=jnp.int32)[None, :] // LANES == src).astype(BF16)
    spread3, select3 = jnp.tile(spread, (3, 1)), jnp.tile(select, (3, 1))
    d_chan = jnp.repeat(d_skip.astype(F32), SSM_HEAD_DIM).reshape(SSM_GROUPS, 1, GROUP_X)

    def lanes(width, idx):
        return pl.BlockSpec((1, cs, width), idx)

    in_specs = [
        lanes(GROUP_X, lambda n, g, c: (n, c, g)),
        lanes(SSM_D_STATE, lambda n, g, c: (n, c, nb + g)),
        lanes(SSM_D_STATE, lambda n, g, c: (n, c, nb + SSM_GROUPS + g)),
        pl.BlockSpec((cs, LANES), lambda n, g, c: (n * nc + c, g)),
        pl.BlockSpec((1, 1, LANES), lambda n, g, c: (g, 0, 0)),
        pl.BlockSpec((1, 1, LANES), lambda n, g, c: (g, 0, 0)),
        pl.BlockSpec((1, 1, GROUP_X), lambda n, g, c: (g, 0, 0)),
        _resident(spread3.shape),
        _resident(select3.shape),
    ]
    return pl.pallas_call(
        _ssd_kernel,
        out_shape=[jax.ShapeDtypeStruct((n_seq, seq, D_INNER), BF16),
                   jax.ShapeDtypeStruct((n_seq, SSM_HEADS, SSM_HEAD_DIM, SSM_D_STATE), F32)],
        grid=(n_seq, SSM_GROUPS, nc),
        in_specs=in_specs,
        out_specs=[pl.BlockSpec((1, cs, GROUP_X), lambda n, g, c: (n, c, g)),
                   pl.BlockSpec((1, HEADS_PER_GROUP, SSM_HEAD_DIM, SSM_D_STATE), lambda n, g, c: (n, g, 0, 0))],
        compiler_params=_cparams("parallel", "parallel", "arbitrary"),
        name="ssd_prompt",
    )(xbc, xbc, xbc, dt, par(dt_bias), par(a_log), d_chan, spread3, select3)


def _ssm_step_kernel(xbc_ref, cst_ref, dt_ref, w_ref, b_ref, dtb_ref, alog_ref, dsk_ref, st_ref,
                     y_ref, sto_ref):
    w = w_ref[...]
    acc = b_ref[...] + w[SSM_CONV_W - 1:SSM_CONV_W, :] * xbc_ref[0]
    for k in range(SSM_CONV_W - 1):
        acc = acc + w[k:k + 1, :] * cst_ref[0, k:k + 1, :]
    act = _silu(acc)
    nb = D_INNER
    row0 = lax.broadcasted_iota(jnp.int32, (SUBLANES, 1), 0) == 0
    hp = lax.Precision.HIGHEST
    for g in range(SSM_GROUPS):
        xs = act[:, g * GROUP_X:(g + 1) * GROUP_X]
        bm = act[:, nb + g * SSM_D_STATE:nb + (g + 1) * SSM_D_STATE]
        cm = act[:, nb + (SSM_GROUPS + g) * SSM_D_STATE:nb + (SSM_GROUPS + g + 1) * SSM_D_STATE]
        dt = jax.nn.softplus(dt_ref[0, :, g * LANES:(g + 1) * LANES] + dtb_ref[g])
        decay = jnp.exp(dt * (-jnp.exp(alog_ref[g])))
        xdt = jnp.concatenate(
            [xs[:, e * SSM_HEAD_DIM:(e + 1) * SSM_HEAD_DIM] * dt[:, e:e + 1] for e in range(HEADS_PER_GROUP)], axis=1)
        x8 = jnp.where(row0, xdt, 0.0)
        b8 = jnp.where(row0, bm, 0.0)
        c8 = jnp.where(row0, cm, 0.0)
        outer = lax.dot_general(x8, b8, (((0,), (0,)), ((), ())), precision=hp,
                                preferred_element_type=F32)
        new = []
        for e in range(HEADS_PER_GROUP):
            head = g * HEADS_PER_GROUP + e
            h_new = st_ref[0, head] * decay[:, e:e + 1] + outer[e * SSM_HEAD_DIM:(e + 1) * SSM_HEAD_DIM, :]
            sto_ref[0, head] = h_new
            new.append(h_new)
        h_g = jnp.concatenate(new, axis=0)
        y8 = lax.dot_general(c8, h_g, (((1,), (1,)), ((), ())), precision=hp,
                             preferred_element_type=F32)
        dsk = dsk_ref[g]
        dx = jnp.concatenate(
            [xs[:, e * SSM_HEAD_DIM:(e + 1) * SSM_HEAD_DIM] * dsk[:, e:e + 1] for e in range(HEADS_PER_GROUP)], axis=1)
        y_ref[0, :, g * GROUP_X:(g + 1) * GROUP_X] = y8[0:1, :] + dx


def _ssm_step(xbc, dt, conv_state, ssm_state, conv_w, conv_b, dt_bias, a_log, d_skip):
    n = xbc.shape[0]
    par = _group_lanes
    st_spec = pl.BlockSpec((1, SSM_HEADS, SSM_HEAD_DIM, SSM_D_STATE), lambda s: (s, 0, 0, 0))
    y, st = pl.pallas_call(
        _ssm_step_kernel,
        out_shape=[jax.ShapeDtypeStruct((n, 1, D_INNER), F32),
                   jax.ShapeDtypeStruct(ssm_state.shape, F32)],
        grid=(n,),
        in_specs=[pl.BlockSpec((1, 1, SSM_CONV_DIM), lambda s: (s, 0, 0)),
                  pl.BlockSpec((1, SSM_CONV_W - 1, SSM_CONV_DIM), lambda s: (s, 0, 0)),
                  pl.BlockSpec((1, 1, SSM_GROUPS * LANES), lambda s: (s, 0, 0)),
                  _resident((SSM_CONV_W, SSM_CONV_DIM)),
                  _resident((1, SSM_CONV_DIM)),
                  _resident((SSM_GROUPS, 1, LANES)),
                  _resident((SSM_GROUPS, 1, LANES)),
                  _resident((SSM_GROUPS, 1, LANES)),
                  st_spec],
        out_specs=[pl.BlockSpec((1, 1, D_INNER), lambda s: (s, 0, 0)), st_spec],
        compiler_params=_cparams("parallel"),
        name="ssm_step",
    )(xbc.reshape(n, 1, SSM_CONV_DIM), conv_state, dt.reshape(n, 1, SSM_GROUPS * LANES), conv_w,
      conv_b.reshape(1, SSM_CONV_DIM), par(dt_bias), par(a_log), par(d_skip), ssm_state)
    return y.reshape(n, D_INNER), st


MOBA_HEADS_PER_STEP = 2


def _moba_kernel(q_ref, k_ref, vt_ref, mean_ref, o_ref, bias_ref):
    i = pl.program_id(2)
    blk = MOBA_BLOCK
    n_blk = mean_ref.shape[1]
    n_heads = q_ref.shape[2] // HEAD_DIM
    sub = lax.broadcasted_iota(jnp.int32, (n_blk, blk), 0)
    sub_f = sub.astype(F32)
    key = lax.broadcasted_iota(jnp.int32, (blk, blk), 0)
    qry = lax.broadcasted_iota(jnp.int32, (blk, blk), 1)
    own = pl.multiple_of(i * blk, blk)
    heads = [slice(hd * HEAD_DIM, (hd + 1) * HEAD_DIM) for hd in range(n_heads)]

    qs_t = []
    for hd, lanes in enumerate(heads):
        q_t = q_ref[0, :, lanes].T
        gate = jnp.dot(mean_ref[0, :, lanes], q_t, precision=lax.Precision.HIGHEST,
                       preferred_element_type=F32)
        past = sub < i
        gate = jnp.where(past, gate, NEG_INF)
        bias = jnp.full((n_blk, blk), NEG_INF, F32)
        for _ in range(MOBA_TOPK):
            top = jnp.max(gate, axis=0, keepdims=True)
            first = jnp.min(jnp.where(gate == top, sub_f, float(n_blk)), axis=0, keepdims=True)
            pick = (sub_f == first) & past
            bias = jnp.where(pick, 0.0, bias)
            gate = jnp.where(pick, NEG_INF, gate)
        for j in range(n_blk):
            bias_ref[hd, j] = bias[j:j + 1, :]
        qs_t.append((q_t * (HEAD_DIM ** -0.5)).astype(BF16))

    def pair_scores(hd, j0):
        off = j0 * blk if isinstance(j0, int) else pl.multiple_of(j0 * blk, 2 * blk)
        s = _dot(k_ref[0, pl.ds(off, 2 * blk), heads[hd]], qs_t[hd])
        return s[:blk] + bias_ref[hd, j0], s[blk:] + bias_ref[hd, j0 + 1]

    s_own = [_dot(k_ref[0, pl.ds(own, blk), lanes], qs_t[hd]) for hd, lanes in enumerate(heads)]
    s_pair = [pair_scores(hd, 0) for hd in range(n_heads)]
    carry = []
    for hd, lanes in enumerate(heads):
        s = jnp.where(key <= qry, s_own[hd], NEG_INF)
        m = jnp.max(s, axis=0, keepdims=True)
        p = jnp.exp(s - m)
        l = jnp.sum(p, axis=0, keepdims=True)
        acc = _dot(vt_ref[0, i, lanes, :], p.astype(BF16))
        carry.append((m, l, acc) + s_pair[hd])

    def body(jj, carry):
        j0 = 2 * jj
        j_next = jnp.minimum(j0 + 2, n_blk - 2)
        s_next = [pair_scores(hd, j_next) for hd in range(n_heads)]
        out = []
        for hd, lanes in enumerate(heads):
            m, l, acc, s0, s1 = carry[hd]
            m_new = jnp.maximum(m, jnp.max(jnp.maximum(s0, s1), axis=0, keepdims=True))
            alpha = jnp.exp(m - m_new)
            p0 = jnp.exp(s0 - m_new)
            p1 = jnp.exp(s1 - m_new)
            l = alpha * l + jnp.sum(p0 + p1, axis=0, keepdims=True)
            v_pair = jnp.concatenate([vt_ref[0, j0, lanes, :], vt_ref[0, j0 + 1, lanes, :]], axis=1)
            p_pair = jnp.concatenate([p0.astype(BF16), p1.astype(BF16)], axis=0)
            acc = alpha * acc + _dot(v_pair, p_pair)
            out.append((m_new, l, acc) + s_next[hd])
        return tuple(out)

    carry = lax.fori_loop(0, (i + 1) // 2, body, tuple(carry))
    for hd, lanes in enumerate(heads):
        m, l, acc = carry[hd][:3]
        o_ref[0, :, lanes] = (acc / l).T.astype(o_ref.dtype)


def _moba_prompt(q, k16, v_t, kmean, n_seq, seq):
    n_blk = seq // MOBA_BLOCK
    width = MOBA_HEADS_PER_STEP * HEAD_DIM
    tile = pl.BlockSpec((1, MOBA_BLOCK, width), lambda n, h, i: (n, i, h))
    return pl.pallas_call(
        _moba_kernel,
        out_shape=jax.ShapeDtypeStruct((n_seq, seq, D_MODEL), BF16),
        grid=(n_seq, N_HEADS // MOBA_HEADS_PER_STEP, n_blk),
        in_specs=[tile,
                  pl.BlockSpec((1, seq, width), lambda n, h, i: (n, 0, h)),
                  pl.BlockSpec((1, n_blk, width, MOBA_BLOCK), lambda n, h, i: (n, 0, h, 0)),
                  pl.BlockSpec((1, n_blk, width), lambda n, h, i: (n, 0, h))],
        out_specs=tile,
        scratch_shapes=[pltpu.VMEM((MOBA_HEADS_PER_STEP, n_blk, 1, MOBA_BLOCK), F32)],
        compiler_params=_cparams("parallel", "parallel", "arbitrary"),
        name="moba_prompt",
    )(q, k16, v_t, kmean)


MEAN_BLOCKS_PER_STEP = 4


def _page_mean_kernel(pt_ref, *refs):
    del pt_ref
    o_ref = refs[-1]
    for b in range(MEAN_BLOCKS_PER_STEP):
        total = jnp.sum(refs[PAGES_PER_BLOCK * b][0], axis=0)
        for t in range(1, PAGES_PER_BLOCK):
            total = total + jnp.sum(refs[PAGES_PER_BLOCK * b + t][0], axis=0)
        o_ref[0, b] = total * (1.0 / MOBA_BLOCK)


def _page_means(cache_k, page_table):
    n_seq, n_pages = page_table.shape
    n_blk = n_pages // PAGES_PER_BLOCK
    per_step = MEAN_BLOCKS_PER_STEP * PAGES_PER_BLOCK
    page = lambda t: pl.BlockSpec(
        (1, PAGE_SIZE, N_HEADS, HEAD_DIM), lambda s, b, pt: (pt[s * n_pages + per_step * b + t], 0, 0, 0))
    out = pl.pallas_call(
        _page_mean_kernel,
        out_shape=jax.ShapeDtypeStruct((n_seq, n_blk, N_HEADS, HEAD_DIM), F32),
        grid_spec=pltpu.PrefetchScalarGridSpec(
            num_scalar_prefetch=1,
            grid=(n_seq, n_blk // MEAN_BLOCKS_PER_STEP),
            in_specs=[page(t) for t in range(per_step)],
            out_specs=pl.BlockSpec((1, MEAN_BLOCKS_PER_STEP, N_HEADS, HEAD_DIM), lambda s, b, pt: (s, b, 0, 0)),
        ),
        compiler_params=_cparams("parallel", "parallel"),
        name="page_means",
    )(page_table.reshape(-1), *([cache_k] * per_step))
    return out.reshape(n_seq, n_blk, D_MODEL)


def _select_kernel(q_ref, mean_ref, idx_ref):
    n_blk = mean_ref.shape[1]
    prod = mean_ref[0] * q_ref[0]
    lane = lax.broadcasted_iota(jnp.int32, (n_blk, LANES), 1)
    sub = lax.broadcasted_iota(jnp.int32, (n_blk, LANES), 0)
    gate = jnp.zeros((n_blk, LANES), F32)
    for hd in range(N_HEADS):
        g_h = jnp.sum(prod[:, hd * HEAD_DIM:(hd + 1) * HEAD_DIM], axis=1, keepdims=True)
        gate = jnp.where(lane == hd, g_h, gate)
    out_sub = lax.broadcasted_iota(jnp.int32, (SUBLANES, LANES), 0)
    out = jnp.zeros((SUBLANES, LANES), jnp.int32)
    for j in range(n_blk):
        g_j = gate[j:j + 1, :]
        beats = (gate > g_j) | ((gate == g_j) & (sub < j))
        rank = jnp.sum(jnp.where(beats, 1.0, 0.0), axis=0, keepdims=True)
        for r in range(MOBA_TOPK):
            out = jnp.where((out_sub == r) & (rank == float(r)), j, out)
    idx_ref[0] = out


def _select_blocks(q, means):
    n_seq, n_blk, _ = means.shape
    idx = pl.pallas_call(
        _select_kernel,
        out_shape=jax.ShapeDtypeStruct((n_seq, SUBLANES, LANES), jnp.int32),
        grid=(n_seq,),
        in_specs=[pl.BlockSpec((1, 1, D_MODEL), lambda s: (s, 0, 0)),
                  pl.BlockSpec((1, n_blk, D_MODEL), lambda s: (s, 0, 0))],
        out_specs=pl.BlockSpec((1, SUBLANES, LANES), lambda s: (s, 0, 0)),
        compiler_params=_cparams("parallel"),
        name="select_blocks",
    )(q.reshape(n_seq, 1, D_MODEL), means)
    return idx[:, :MOBA_TOPK, :N_HEADS]


N_SEL_PAGES = MOBA_TOPK * PAGES_PER_BLOCK


def _decode_attn_kernel(pt_ref, idx_ref, q_ref, kn_ref, vn_ref, k_hbm, v_hbm, o_ref, kbuf, vbuf, sems,
                        *, n_pages):
    s_i = pl.program_id(0)

    def copies(hd, slot):
        r, t = divmod(slot, PAGES_PER_BLOCK)
        blk = idx_ref[(s_i * MOBA_TOPK + r) * N_HEADS + hd]
        pg = pt_ref[s_i * n_pages + PAGES_PER_BLOCK * blk + t]
        return (pltpu.make_async_copy(k_hbm.at[pg, :, hd, :], kbuf.at[hd, slot], sems.at[0, hd, slot]),
                pltpu.make_async_copy(v_hbm.at[pg, :, hd, :], vbuf.at[hd, slot], sems.at[1, hd, slot]))

    for hd in range(N_HEADS):
        for slot in range(N_SEL_PAGES):
            for cp in copies(hd, slot):
                cp.start()

    for hd in range(N_HEADS):
        lanes = slice(hd * HEAD_DIM, (hd + 1) * HEAD_DIM)
        for slot in range(N_SEL_PAGES):
            for cp in copies(hd, slot):
                cp.wait()
        q = q_ref[0, :, lanes] * (HEAD_DIM ** -0.5)
        s_own = jnp.sum(q * kn_ref[0, :, lanes], axis=1, keepdims=True)
        scores = [jnp.sum(kbuf[hd, slot] * q, axis=1, keepdims=True) for slot in range(N_SEL_PAGES)]
        m = s_own
        for s in scores:
            m = jnp.maximum(m, jnp.max(s, axis=0, keepdims=True))
        p_own = jnp.exp(s_own - m)
        l = p_own
        acc = p_own * vn_ref[0, :, lanes]
        for slot, s in enumerate(scores):
            p = jnp.exp(s - m)
            l = l + jnp.sum(p, axis=0, keepdims=True)
            acc = acc + jnp.sum(p * vbuf[hd, slot], axis=0, keepdims=True)
        o_ref[0, :, lanes] = acc / l


def _decode_attn(q, k_new, v_new, cache_k, cache_v, page_table, idx):
    n_seq, n_pages = page_table.shape
    vec = pl.BlockSpec((1, 1, D_MODEL), lambda s, pt, sel: (s, 0, 0))
    hbm = pl.BlockSpec(memory_space=pl.ANY)
    r3 = lambda a: a.reshape(n_seq, 1, D_MODEL)
    slabs = (N_HEADS, N_SEL_PAGES, PAGE_SIZE, HEAD_DIM)
    out = pl.pallas_call(
        functools.partial(_decode_attn_kernel, n_pages=n_pages),
        out_shape=jax.ShapeDtypeStruct((n_seq, 1, D_MODEL), F32),
        grid_spec=pltpu.PrefetchScalarGridSpec(
            num_scalar_prefetch=2,
            grid=(n_seq,),
            in_specs=[vec, vec, vec, hbm, hbm],
            out_specs=vec,
            scratch_shapes=[pltpu.VMEM(slabs, F32), pltpu.VMEM(slabs, F32),
                            pltpu.SemaphoreType.DMA((2, N_HEADS, N_SEL_PAGES))],
        ),
        compiler_params=_cparams("arbitrary"),
        name="decode_attn",
    )(page_table.reshape(-1), idx.reshape(-1), r3(q), r3(k_new), r3(v_new), cache_k, cache_v)
    return out.reshape(n_seq, D_MODEL)


def _cast_kernel(w_ref, o_ref):
    o_ref[...] = w_ref[...].astype(o_ref.dtype)


def _to_bf16(w, layer, row_tile, n_cols=None, col_tile=None):
    _, rows, cols = w.shape
    n_cols = n_cols or cols
    col_tile = col_tile or n_cols
    return pl.pallas_call(
        _cast_kernel,
        out_shape=jax.ShapeDtypeStruct((rows, n_cols), BF16),
        grid=(rows // row_tile, n_cols // col_tile),
        in_specs=[pl.BlockSpec((None, row_tile, col_tile), lambda i, j: (layer, i, j))],
        out_specs=pl.BlockSpec((row_tile, col_tile), lambda i, j: (i, j)),
        compiler_params=_cparams("parallel", "parallel"),
        name="to_bf16",
    )(w)


def _dt_weight(w_in):
    w_dt = w_in[:, D_INNER + SSM_CONV_DIM:].reshape(D_MODEL, SSM_GROUPS, HEADS_PER_GROUP)
    w_dt = jnp.pad(w_dt, ((0, 0), (0, 0), (0, LANES - HEADS_PER_GROUP)))
    return w_dt.reshape(D_MODEL, SSM_GROUPS * LANES).astype(BF16)


W_ROW_TILE = 256
W_DOWN_ROW_TILE = D_FF // 8


def kernel(x_prompt, x_sample, state_conv, state_ssm, cache_k, cache_v, page_table, norm_ffn_a, w_ffn_a_up, w_ffn_a_down, norm_mix, norm_ffn_b, w_ffn_b_up, w_ffn_b_down, ssm_w_in, ssm_conv_w, ssm_conv_b, ssm_dt_bias, ssm_a_log, ssm_d, ssm_norm, ssm_w_out, norm_kv, w_kv, w_q, w_o, norm_final):
    n_p, seq, d = x_prompt.shape
    n_s = x_sample.shape[0]
    past_len = page_table.shape[1] * PAGE_SIZE
    tm_p = 512
    tm_s = n_s

    hp = x_prompt.reshape(n_p * seq, d)
    hs = x_sample.reshape(n_s, d)
    tab_p = _rope_tables(jnp.arange(seq, dtype=jnp.int32))
    tab_s = _rope_tables(jnp.full((n_s,), past_len, jnp.int32))

    conv_p, ssm_p, conv_s, ssm_s = [], [], [], []
    k_p = v_p = k_s = v_s = None
    for layer in range(DEPTH):
        if layer == N_A_LAYERS:
            w_kv16 = _to_bf16(w_kv[None], 0, W_ROW_TILE)
            k_p, v_p, kmean_p, k16, v_t = _kv_proj(hp, norm_kv, w_kv16, tab_p, MOBA_BLOCK, True)
            k_s, v_s = _kv_proj(hs, norm_kv, w_kv16, tab_s, tm_s, False)
            k16 = k16.reshape(n_p, seq, d)
            v_t = v_t.reshape(n_p, seq // MOBA_BLOCK, d, MOBA_BLOCK)
            kmean_p = kmean_p.reshape(n_p, seq // MOBA_BLOCK, d)
            kmean_s = _page_means(cache_k, page_table)

        w_up = _to_bf16(w_ffn_a_up, layer, W_ROW_TILE)
        w_dn = _to_bf16(w_ffn_a_down, layer, W_DOWN_ROW_TILE)
        hp = _ffn(hp, norm_ffn_a[layer], w_up, w_dn, tm_p)
        hs = _ffn(hs, norm_ffn_a[layer], w_up, w_dn, tm_s)

        if layer < N_A_LAYERS:
            w_in = _to_bf16(ssm_w_in, layer, W_ROW_TILE, D_INNER + SSM_CONV_DIM, D_MODEL)
            w_dt = _dt_weight(ssm_w_in[layer])
            w_out = _to_bf16(ssm_w_out, layer, W_ROW_TILE)
            ssm_args = (ssm_conv_w[layer], ssm_conv_b[layer], ssm_dt_bias[layer], ssm_a_log[layer], ssm_d[layer])
            z, xbc, dt, tail = _in_proj(hp, norm_mix[layer], w_in, w_dt, ssm_conv_w[layer], ssm_conv_b[layer],
                                        tm_p, seq)
            y, st = _ssd_prompt(xbc, dt, *ssm_args[2:], n_p, seq)
            conv_p.append(tail[:, SUBLANES - (SSM_CONV_W - 1):])
            ssm_p.append(st)
            hp = _out_proj(y.reshape(n_p * seq, D_INNER), w_out, hp, tm_p, z=z, norm_w=ssm_norm[layer])

            z, xbc = _proj(hs, norm_mix[layer], w_in, (D_INNER, SSM_CONV_DIM), tm_s)
            dt, = _proj(hs, norm_mix[layer], w_dt, (SSM_GROUPS * LANES,), tm_s)
            y, st = _ssm_step(xbc, dt, state_conv[layer], state_ssm[layer], *ssm_args)
            conv_s.append(jnp.concatenate([state_conv[layer][:, 1:], xbc[:, None, :]], axis=1))
            ssm_s.append(st)
            hs = _out_proj(y, w_out, hs, tm_s, z=z, norm_w=ssm_norm[layer])
        else:
            j = layer - N_A_LAYERS
            w_q16, w_o16 = _to_bf16(w_q, j, W_ROW_TILE), _to_bf16(w_o, j, W_ROW_TILE)
            q = _q_proj(hp, norm_mix[layer], w_q16, tab_p, tm_p)
            o = _moba_prompt(q.reshape(n_p, seq, d), k16, v_t, kmean_p, n_p, seq)
            hp = _out_proj(o.reshape(n_p * seq, d), w_o16, hp, tm_p)

            q = _q_proj(hs, norm_mix[layer], w_q16, tab_s, tm_s)
            idx = _select_blocks(q, kmean_s)
            o = _decode_attn(q, k_s, v_s, cache_k, cache_v, page_table, idx)
            hs = _out_proj(o, w_o16, hs, tm_s)

        w_up = _to_bf16(w_ffn_b_up, layer, W_ROW_TILE)
        w_dn = _to_bf16(w_ffn_b_down, layer, W_DOWN_ROW_TILE)
        final_g = norm_final if layer == DEPTH - 1 else None
        hp = _ffn(hp, norm_ffn_b[layer], w_up, w_dn, tm_p, final_g)
        hs = _ffn(hs, norm_ffn_b[layer], w_up, w_dn, tm_s, final_g)

    heads = (N_HEADS, HEAD_DIM)
    return (hp.reshape(n_p, seq, d), hs.reshape(n_s, 1, d),
            jnp.stack(conv_p), jnp.stack(ssm_p),
            k_p.reshape(n_p, seq, *heads), v_p.reshape(n_p, seq, *heads),
            jnp.stack(conv_s), jnp.stack(ssm_s),
            k_s.reshape(n_s, 1, *heads), v_s.reshape(n_s, 1, *heads))
```

```python
import functools
import math

import jax
import jax.numpy as jnp
from jax import lax
from jax.experimental import pallas as pl
from jax.experimental.pallas import tpu as pltpu

F32 = jnp.float32
BF16 = jnp.bfloat16

D_MODEL = 1024
DEPTH = 4
N_A_LAYERS = 2
NORM_EPS = 1e-5
D_FF = 2816
D_INNER = 2048
SSM_HEAD_DIM = 64
SSM_HEADS = 32
SSM_GROUPS = 4
HEADS_PER_GROUP = 8
SSM_D_STATE = 128
SSM_CONV_W = 4
SSM_CONV_DIM = D_INNER + 2 * SSM_GROUPS * SSM_D_STATE
SSD_CHUNK = 128
GROUP_X = HEADS_PER_GROUP * SSM_HEAD_DIM
N_HEADS = 8
HEAD_DIM = 128
ROT_DIM = 32
ROPE_THETA = 500000.0
MOBA_BLOCK = 256
MOBA_TOPK = 3
PAGE_SIZE = 128
PAGES_PER_BLOCK = MOBA_BLOCK // PAGE_SIZE

LANES = 128
SUBLANES = 8
VMEM_LIMIT_BYTES = 56 * 1024 * 1024
FF_CHUNK = 256
NEG_INF = float("-inf")


def _cparams(*sem):
    return pltpu.CompilerParams(dimension_semantics=sem, vmem_limit_bytes=VMEM_LIMIT_BYTES)


def _resident(shape):
    nd = len(shape)
    return pl.BlockSpec(shape, lambda *_: (0,) * nd, pipeline_mode=pl.Buffered(1))


def _rms(x, g):
    return x * lax.rsqrt(jnp.mean(x * x, axis=-1, keepdims=True) + NORM_EPS) * g


def _silu(x):
    return x * jax.nn.sigmoid(x)


def _split3(x):
    hi = x.astype(BF16)
    r1 = x - hi.astype(F32)
    mid = r1.astype(BF16)
    lo = (r1 - mid.astype(F32)).astype(BF16)
    return hi, mid, lo


def _dot(a, b):
    return jnp.dot(a, b, preferred_element_type=F32)


def _dot_nt(a, b):
    return lax.dot_general(a, b, (((1,), (1,)), ((), ())), preferred_element_type=F32)


def _ffn_kernel(*refs, mixer, final_norm):
    refs = list(refs)
    o_ref = refs.pop()
    gf_ref = refs.pop() if final_norm else None
    x_ref, g_ref, wup_ref, wdn_ref = refs[:4]
    x = x_ref[...]
    if mixer == "gated":
        y_ref, wo_ref, z_ref, nw_ref = refs[4:]
        y = _rms(y_ref[...].astype(F32) * _silu(z_ref[...].astype(F32)), nw_ref[...])
        x = x + _dot(y.astype(BF16), wo_ref[...])
    elif mixer == "plain":
        y_ref, wo_ref = refs[4:]
        x = x + _dot(y_ref[...].astype(BF16), wo_ref[...])
    xn = _rms(x, g_ref[...]).astype(BF16)
    acc = jnp.zeros_like(x)
    for c in range(D_FF // FF_CHUNK):
        lo, hi = c * FF_CHUNK, (c + 1) * FF_CHUNK
        gate = _dot(xn, wup_ref[:, lo:hi])
        up = _dot(xn, wup_ref[:, D_FF + lo:D_FF + hi])
        act = (_silu(gate) * up).astype(BF16)
        acc = acc + _dot(act, wdn_ref[lo:hi, :])
    h = x + 0.5 * acc
    if final_norm:
        h = _rms(h, gf_ref[...])
    o_ref[...] = h


def _ffn(h, g, w_up, w_dn, tm, final_g=None, mix=None):
    t, d = h.shape
    row = pl.BlockSpec((tm, d), lambda i: (i, 0))
    in_specs = [row, _resident((1, d)), _resident(w_up.shape), _resident(w_dn.shape)]
    args = [h, g.reshape(1, d), w_up, w_dn]
    mixer = None
    if mix is not None:
        y, w_o = mix[:2]
        k = y.shape[1]
        row_k = pl.BlockSpec((tm, k), lambda i: (i, 0))
        in_specs += [row_k, _resident(w_o.shape)]
        args += [y, w_o]
        mixer = "plain"
        if len(mix) == 4:
            in_specs += [row_k, _resident((1, k))]
            args += [mix[2], mix[3].reshape(1, k)]
            mixer = "gated"
    if final_g is not None:
        in_specs.append(_resident((1, d)))
        args.append(final_g.reshape(1, d))
    return pl.pallas_call(
        functools.partial(_ffn_kernel, mixer=mixer, final_norm=final_g is not None),
        out_shape=jax.ShapeDtypeStruct((t, d), F32),
        grid=(t // tm,),
        in_specs=in_specs,
        out_specs=row,
        compiler_params=_cparams("parallel"),
        name="ffn",
    )(*args)


PROJ_CHUNK = 512


def _store_proj(xn, w_ref, o_ref, col0, width):
    for c0 in range(0, width, PROJ_CHUNK):
        c1 = min(c0 + PROJ_CHUNK, width)
        o_ref[:, c0:c1] = _dot(xn, w_ref[:, col0 + c0:col0 + c1]).astype(o_ref.dtype)


def _proj_kernel(x_ref, g_ref, w_ref, *o_refs):
    xn = _rms(x_ref[...], g_ref[...]).astype(BF16)
    col = 0
    for o_ref in o_refs:
        width = o_ref.shape[1]
        _store_proj(xn, w_ref, o_ref, col, width)
        col += width


def _proj(h, g, w, widths, tm):
    t, d = h.shape
    return pl.pallas_call(
        _proj_kernel,
        out_shape=[jax.ShapeDtypeStruct((t, n), F32) for n in widths],
        grid=(t // tm,),
        in_specs=[pl.BlockSpec((tm, d), lambda i: (i, 0)), _resident((1, d)), _resident(w.shape)],
        out_specs=[pl.BlockSpec((tm, n), lambda i: (i, 0)) for n in widths],
        compiler_params=_cparams("parallel"),
        name="proj",
    )(h, g.reshape(1, d), w)


def _rope(y, cos, sin_lo, sin_hi):
    outs = []
    for hd in range(y.shape[1] // HEAD_DIM):
        xh = y[:, hd * HEAD_DIM:(hd + 1) * HEAD_DIM]
        up = pltpu.roll(xh, HEAD_DIM - ROT_DIM // 2, axis=1)
        dn = pltpu.roll(xh, ROT_DIM // 2, axis=1)
        outs.append(xh * cos + up * sin_lo + dn * sin_hi)
    return jnp.concatenate(outs, axis=1)


def _rope_tables(pos):
    half = ROT_DIM // 2
    inv = ROPE_THETA ** (-2.0 * jnp.arange(half, dtype=F32) / ROT_DIM)
    ang = pos.astype(F32)[:, None] * inv[None, :]
    cos, sin = jnp.cos(ang), jnp.sin(ang)
    n = pos.shape[0]
    ones = jnp.ones((n, HEAD_DIM - ROT_DIM), F32)
    zeros = jnp.zeros((n, HEAD_DIM - half), F32)
    cos_t = jnp.concatenate([cos, cos, ones], axis=1)
    sin_lo = jnp.concatenate([-sin, zeros], axis=1)
    sin_hi = jnp.concatenate([jnp.zeros((n, half), F32), sin, jnp.zeros((n, HEAD_DIM - ROT_DIM), F32)], axis=1)
    return cos_t, sin_lo, sin_hi


def _q_kernel(x_ref, g_ref, w_ref, cos_ref, slo_ref, shi_ref, q_ref):
    xn = _rms(x_ref[...], g_ref[...]).astype(BF16)
    q_ref[...] = _rope(_dot(xn, w_ref[...]), cos_ref[...], slo_ref[...], shi_ref[...])


def _q_proj(h, g, w, tables, tm):
    t, d = h.shape
    n_tab = tables[0].shape[0] // tm
    tab = pl.BlockSpec((tm, HEAD_DIM), lambda i: (i % n_tab, 0))
    row = pl.BlockSpec((tm, d), lambda i: (i, 0))
    return pl.pallas_call(
        _q_kernel,
        out_shape=jax.ShapeDtypeStruct((t, d), F32),
        grid=(t // tm,),
        in_specs=[row, _resident((1, d)), _resident(w.shape), tab, tab, tab],
        out_specs=row,
        compiler_params=_cparams("parallel"),
        name="q_proj",
    )(h, g.reshape(1, d), w, *tables)


def _kv_kernel(x_ref, g_ref, w_ref, cos_ref, slo_ref, shi_ref, k_ref, v_ref, *block_refs, block_sums):
    xn = _rms(x_ref[...], g_ref[...]).astype(BF16)
    k = _rope(_dot(xn, w_ref[:, :D_MODEL]), cos_ref[...], slo_ref[...], shi_ref[...])
    v = _dot(xn, w_ref[:, D_MODEL:])
    k_ref[...] = k
    v_ref[...] = v
    if block_sums:
        ksum_ref, k16_ref, vt16_ref = block_refs
        ksum_ref[0] = jnp.sum(k, axis=0, keepdims=True) * (1.0 / MOBA_BLOCK)
        k16_ref[...] = k.astype(BF16)
        vt16_ref[0] = v.T.astype(BF16)


def _kv_proj(h, g, w, tables, tm, block_sums):
    t, d = h.shape
    n_tab = tables[0].shape[0] // tm
    tab = pl.BlockSpec((tm, HEAD_DIM), lambda i: (i % n_tab, 0))
    row = pl.BlockSpec((tm, d), lambda i: (i, 0))
    out_shape = [jax.ShapeDtypeStruct((t, d), F32), jax.ShapeDtypeStruct((t, d), F32)]
    out_specs = [row, row]
    if block_sums:
        assert tm == MOBA_BLOCK
        out_shape += [jax.ShapeDtypeStruct((t // tm, 1, d), F32), jax.ShapeDtypeStruct((t, d), BF16),
                      jax.ShapeDtypeStruct((t // tm, d, tm), BF16)]
        out_specs += [pl.BlockSpec((1, 1, d), lambda i: (i, 0, 0)), row,
                      pl.BlockSpec((1, d, tm), lambda i: (i, 0, 0))]
    return pl.pallas_call(
        functools.partial(_kv_kernel, block_sums=block_sums),
        out_shape=out_shape,
        grid=(t // tm,),
        in_specs=[row, _resident((1, d)), _resident(w.shape), tab, tab, tab],
        out_specs=out_specs,
        compiler_params=_cparams("parallel"),
        name="kv_proj",
    )(h, g.reshape(1, d), w, *tables)


def _in_proj_kernel(x_ref, g_ref, w_ref, wdt_ref, cw_ref, cb_ref, z_ref, xbc_ref, dt_ref, tail_ref, ext_ref,
                    *, tiles_per_seq):
    tm = x_ref.shape[0]
    xn = _rms(x_ref[...], g_ref[...]).astype(BF16)

    @pl.when(pl.program_id(0) % tiles_per_seq == 0)
    def _():
        ext_ref[...] = jnp.zeros((SUBLANES, SSM_CONV_DIM), F32)

    first_rows = lax.broadcasted_iota(jnp.int32, (SUBLANES, PROJ_CHUNK), 0)

    def xbc_chunk(c0):
        return _dot(xn, w_ref[:, D_INNER + c0:D_INNER + c0 + PROJ_CHUNK])

    other = [(z_ref, c0, w_ref) for c0 in range(0, D_INNER, PROJ_CHUNK)]
    other.append((dt_ref, 0, wdt_ref))
    pending = xbc_chunk(0)
    for c0 in range(0, SSM_CONV_DIM, PROJ_CHUNK):
        cols = slice(c0, c0 + PROJ_CHUNK)
        raw = pending
        if c0 + PROJ_CHUNK < SSM_CONV_DIM:
            pending = xbc_chunk(c0 + PROJ_CHUNK)
        if other:
            o_ref, dst, src_ref = other.pop(0)
            o_ref[:, dst:dst + PROJ_CHUNK] = _dot(xn, src_ref[:, dst:dst + PROJ_CHUNK]).astype(o_ref.dtype)
        prev = ext_ref[:, cols]
        acc = cb_ref[:, cols] + cw_ref[SSM_CONV_W - 1:SSM_CONV_W, cols] * raw
        for back in range(1, SSM_CONV_W):
            cur = pltpu.roll(raw, back, axis=0)
            head = jnp.where(first_rows < back, pltpu.roll(prev, back, axis=0), cur[:SUBLANES])
            shifted = jnp.concatenate([head, cur[SUBLANES:]], axis=0)
            acc = acc + cw_ref[SSM_CONV_W - 1 - back:SSM_CONV_W - back, cols] * shifted
        xbc_ref[:, cols] = _silu(acc).astype(xbc_ref.dtype)
        tail = raw[tm - SUBLANES:]
        tail_ref[0, :, cols] = tail
        ext_ref[:, cols] = tail


def _in_proj(h, g, w, w_dt, conv_w, conv_b, tm, seq):
    t, d = h.shape
    tiles_per_seq = seq // tm
    row = lambda n: pl.BlockSpec((tm, n), lambda i: (i, 0))
    return pl.pallas_call(
        functools.partial(_in_proj_kernel, tiles_per_seq=tiles_per_seq),
        out_shape=[jax.ShapeDtypeStruct((t, D_INNER), BF16),
                   jax.ShapeDtypeStruct((t, SSM_CONV_DIM), BF16),
                   jax.ShapeDtypeStruct((t, SSM_GROUPS * LANES), F32),
                   jax.ShapeDtypeStruct((t // seq, SUBLANES, SSM_CONV_DIM), F32)],
        grid=(t // tm,),
        in_specs=[row(d), _resident((1, d)), _resident(w.shape), _resident(w_dt.shape),
                  _resident(conv_w.shape), _resident((1, SSM_CONV_DIM))],
        out_specs=[row(D_INNER), row(SSM_CONV_DIM), row(SSM_GROUPS * LANES),
                   pl.BlockSpec((1, SUBLANES, SSM_CONV_DIM), lambda i: (i // tiles_per_seq, 0, 0))],
        scratch_shapes=[pltpu.VMEM((SUBLANES, SSM_CONV_DIM), F32)],
        compiler_params=_cparams("arbitrary"),
        name="in_proj",
    )(h, g.reshape(1, d), w, w_dt, conv_w, conv_b.reshape(1, SSM_CONV_DIM))


def _split_dot(x, w, terms):
    return _dot(jnp.concatenate(_split3(x)[:terms], axis=1), w)


SSD_GROUPS_PER_STEP = 2


def _ssd_kernel(x_ref, b_ref, c_ref, dt_ref, dtb_ref, alog_ref, dsk_ref, spread_ref, select_ref,
                y_ref, st_ref):
    cs = SSD_CHUNK

    @pl.when(pl.program_id(2) == 0)
    def _():
        st_ref[...] = jnp.zeros(st_ref.shape, F32)

    row = lax.broadcasted_iota(jnp.int32, (cs, cs), 0)
    col = lax.broadcasted_iota(jnp.int32, (cs, cs), 1)
    causal = row >= col
    tril = jnp.where(causal, 1.0, 0.0).astype(BF16)
    low_half = col < SSM_HEAD_DIM
    spread = spread_ref[...]

    for g in range(SSD_GROUPS_PER_STEP):
        heads = slice(g * HEADS_PER_GROUP, (g + 1) * HEADS_PER_GROUP)
        xs = x_ref[0, :, g * GROUP_X:(g + 1) * GROUP_X].astype(F32)
        bm16 = b_ref[0, :, g * SSM_D_STATE:(g + 1) * SSM_D_STATE]
        cm16 = c_ref[0, :, g * SSM_D_STATE:(g + 1) * SSM_D_STATE]

        dt = jax.nn.softplus(dt_ref[:, g * LANES:(g + 1) * LANES] + dtb_ref[g])
        a = jnp.where(col < HEADS_PER_GROUP, dt * (-jnp.exp(alog_ref[g])), 0.0)
        a_hi, a_mid, a_lo = _split3(a)
        acum = _dot(tril, a_hi) + _dot(tril, a_mid) + _dot(tril, a_lo)
        acum_t = acum.T
        ac_last = acum[cs - 1:cs, :]

        xdt = xs * _split_dot(dt, spread, 2)
        xdt16 = xdt.astype(BF16)
        xw = xdt * _split_dot(jnp.exp(ac_last - acum), spread, 2)
        ac_col = _split_dot(acum, select_ref[...], 3)

        cb = _dot_nt(cm16, bm16)
        st_old = st_ref[0, heads].reshape(GROUP_X, SSM_D_STATE)
        y = _dot_nt(cm16, st_old.astype(BF16)) * _split_dot(jnp.exp(acum), spread, 2) + dsk_ref[g] * xs

        for pair in range(HEADS_PER_GROUP // 2):
            x_pair = xdt16[:, pair * LANES:(pair + 1) * LANES]
            y_pair = None
            for half in range(2):
                e = 2 * pair + half
                seg = ac_col[:, e * LANES:(e + 1) * LANES] - acum_t[e:e + 1, :]
                lmat = jnp.exp(jnp.where(causal, seg, NEG_INF))
                x_half = jnp.where(low_half if half == 0 else ~low_half, x_pair, jnp.zeros_like(x_pair))
                part = _dot((cb * lmat).astype(BF16), x_half)
                y_pair = part if y_pair is None else y_pair + part
            lanes = slice(g * GROUP_X + pair * LANES, g * GROUP_X + (pair + 1) * LANES)
            y_ref[0, :, lanes] = (y[:, pair * LANES:(pair + 1) * LANES] + y_pair).astype(y_ref.dtype)

        states = _dot(xw.T.astype(BF16), bm16)
        for e in range(HEADS_PER_GROUP):
            lo, hi = e * SSM_HEAD_DIM, (e + 1) * SSM_HEAD_DIM
            head = g * HEADS_PER_GROUP + e
            st_ref[0, head] = st_ref[0, head] * jnp.exp(ac_last[:, e:e + 1]) + states[lo:hi, :]


def _group_lanes(v):
    v = v.astype(F32).reshape(SSM_GROUPS, 1, HEADS_PER_GROUP)
    return jnp.pad(v, ((0, 0), (0, 0), (0, LANES - HEADS_PER_GROUP)))


def _ssd_prompt(xbc, dt, dt_bias, a_log, d_skip, n_seq, seq):
    nc = seq // SSD_CHUNK
    cs = SSD_CHUNK
    xbc = xbc.reshape(n_seq, seq, SSM_CONV_DIM)
    nb = D_INNER // SSM_D_STATE
    par = _group_lanes
    src = jnp.arange(LANES, dtype=jnp.int32)[:, None]
    spread = (jnp.arange(GROUP_X, dtype=jnp.int32)[None, :] // SSM_HEAD_DIM == src).astype(BF16)
    select = (jnp.arange(HEADS_PER_GROUP * LANES, dtype=jnp.int32)[None, :] // LANES == src).astype(BF16)
    spread2, select3 = jnp.tile(spread, (2, 1)), jnp.tile(select, (3, 1))
    d_chan = jnp.repeat(d_skip.astype(F32), SSM_HEAD_DIM).reshape(SSM_GROUPS, 1, GROUP_X)
    gps = SSD_GROUPS_PER_STEP
    b_blk = nb // gps
    c_blk = (nb + SSM_GROUPS) // gps

    def lanes(width, idx):
        return pl.BlockSpec((1, cs, gps * width), idx)

    def per_group(width):
        return pl.BlockSpec((gps, 1, width), lambda n, g, c: (g, 0, 0))

    in_specs = [
        lanes(GROUP_X, lambda n, g, c: (n, c, g)),
        lanes(SSM_D_STATE, lambda n, g, c: (n, c, b_blk + g)),
        lanes(SSM_D_STATE, lambda n, g, c: (n, c, c_blk + g)),
        pl.BlockSpec((cs, gps * LANES), lambda n, g, c: (n * nc + c, g)),
        per_group(LANES),
        per_group(LANES),
        per_group(GROUP_X),
        _resident(spread2.shape),
        _resident(select3.shape),
    ]
    return pl.pallas_call(
        _ssd_kernel,
        out_shape=[jax.ShapeDtypeStruct((n_seq, seq, D_INNER), BF16),
                   jax.ShapeDtypeStruct((n_seq, SSM_HEADS, SSM_HEAD_DIM, SSM_D_STATE), F32)],
        grid=(n_seq, SSM_GROUPS // gps, nc),
        in_specs=in_specs,
        out_specs=[lanes(GROUP_X, lambda n, g, c: (n, c, g)),
                   pl.BlockSpec((1, gps * HEADS_PER_GROUP, SSM_HEAD_DIM, SSM_D_STATE),
                                lambda n, g, c: (n, g, 0, 0))],
        compiler_params=_cparams("parallel", "parallel", "arbitrary"),
        name="ssd_prompt",
    )(xbc, xbc, xbc, dt, par(dt_bias), par(a_log), d_chan, spread2, select3)


def _ssm_step_kernel(xbc_ref, cst_ref, dt_ref, w_ref, b_ref, dtb_ref, alog_ref, dsk_ref, st_ref,
                     y_ref, sto_ref):
    w = w_ref[...]
    acc = b_ref[...] + w[SSM_CONV_W - 1:SSM_CONV_W, :] * xbc_ref[0]
    for k in range(SSM_CONV_W - 1):
        acc = acc + w[k:k + 1, :] * cst_ref[0, k:k + 1, :]
    act = _silu(acc)
    nb = D_INNER
    row0 = lax.broadcasted_iota(jnp.int32, (SUBLANES, 1), 0) == 0
    hp = lax.Precision.HIGHEST
    for g in range(SSM_GROUPS):
        xs = act[:, g * GROUP_X:(g + 1) * GROUP_X]
        bm = act[:, nb + g * SSM_D_STATE:nb + (g + 1) * SSM_D_STATE]
        cm = act[:, nb + (SSM_GROUPS + g) * SSM_D_STATE:nb + (SSM_GROUPS + g + 1) * SSM_D_STATE]
        dt = jax.nn.softplus(dt_ref[0, :, g * LANES:(g + 1) * LANES] + dtb_ref[g])
        decay = jnp.exp(dt * (-jnp.exp(alog_ref[g])))
        xdt = jnp.concatenate(
            [xs[:, e * SSM_HEAD_DIM:(e + 1) * SSM_HEAD_DIM] * dt[:, e:e + 1] for e in range(HEADS_PER_GROUP)], axis=1)
        x8 = jnp.where(row0, xdt, 0.0)
        b8 = jnp.where(row0, bm, 0.0)
        c8 = jnp.where(row0, cm, 0.0)
        outer = lax.dot_general(x8, b8, (((0,), (0,)), ((), ())), precision=hp,
                                preferred_element_type=F32)
        new = []
        for e in range(HEADS_PER_GROUP):
            head = g * HEADS_PER_GROUP + e
            h_new = st_ref[0, head] * decay[:, e:e + 1] + outer[e * SSM_HEAD_DIM:(e + 1) * SSM_HEAD_DIM, :]
            sto_ref[0, head] = h_new
            new.append(h_new)
        h_g = jnp.concatenate(new, axis=0)
        y8 = lax.dot_general(c8, h_g, (((1,), (1,)), ((), ())), precision=hp,
                             preferred_element_type=F32)
        dsk = dsk_ref[g]
        dx = jnp.concatenate(
            [xs[:, e * SSM_HEAD_DIM:(e + 1) * SSM_HEAD_DIM] * dsk[:, e:e + 1] for e in range(HEADS_PER_GROUP)], axis=1)
        y_ref[0, :, g * GROUP_X:(g + 1) * GROUP_X] = y8[0:1, :] + dx


def _ssm_step(xbc, dt, conv_state, ssm_states, layer, conv_w, conv_b, dt_bias, a_log, d_skip):
    n = xbc.shape[0]
    par = _group_lanes
    st_spec = pl.BlockSpec((1, SSM_HEADS, SSM_HEAD_DIM, SSM_D_STATE), lambda s: (s, 0, 0, 0))
    st_in_spec = pl.BlockSpec((None, 1, SSM_HEADS, SSM_HEAD_DIM, SSM_D_STATE), lambda s: (layer, s, 0, 0, 0))
    y, st = pl.pallas_call(
        _ssm_step_kernel,
        out_shape=[jax.ShapeDtypeStruct((n, 1, D_INNER), F32),
                   jax.ShapeDtypeStruct(ssm_states.shape[1:], F32)],
        grid=(n,),
        in_specs=[pl.BlockSpec((1, 1, SSM_CONV_DIM), lambda s: (s, 0, 0)),
                  pl.BlockSpec((1, SSM_CONV_W - 1, SSM_CONV_DIM), lambda s: (s, 0, 0)),
                  pl.BlockSpec((1, 1, SSM_GROUPS * LANES), lambda s: (s, 0, 0)),
                  _resident((SSM_CONV_W, SSM_CONV_DIM)),
                  _resident((1, SSM_CONV_DIM)),
                  _resident((SSM_GROUPS, 1, LANES)),
                  _resident((SSM_GROUPS, 1, LANES)),
                  _resident((SSM_GROUPS, 1, LANES)),
                  st_in_spec],
        out_specs=[pl.BlockSpec((1, 1, D_INNER), lambda s: (s, 0, 0)), st_spec],
        compiler_params=_cparams("parallel"),
        name="ssm_step",
    )(xbc.reshape(n, 1, SSM_CONV_DIM), conv_state, dt.reshape(n, 1, SSM_GROUPS * LANES), conv_w,
      conv_b.reshape(1, SSM_CONV_DIM), par(dt_bias), par(a_log), par(d_skip), ssm_states)
    return y.reshape(n, D_INNER), st


MOBA_HEADS_PER_STEP = 4


def _moba_kernel(q_ref, k_ref, vt_ref, mean_ref, o_ref, bias_ref):
    i = pl.program_id(2)
    blk = MOBA_BLOCK
    n_blk = mean_ref.shape[1]
    n_heads = q_ref.shape[2] // HEAD_DIM
    sub = lax.broadcasted_iota(jnp.int32, (n_blk, blk), 0)
    sub_f = sub.astype(F32)
    key = lax.broadcasted_iota(jnp.int32, (blk, blk), 0)
    qry = lax.broadcasted_iota(jnp.int32, (blk, blk), 1)
    own = pl.multiple_of(i * blk, blk)
    heads = [slice(hd * HEAD_DIM, (hd + 1) * HEAD_DIM) for hd in range(n_heads)]

    qs_t = []
    for hd, lanes in enumerate(heads):
        q_t = q_ref[0, :, lanes].T
        gate = jnp.dot(mean_ref[0, :, lanes], q_t, precision=lax.Precision.HIGHEST,
                       preferred_element_type=F32)
        past = sub < i
        gate = jnp.where(past, gate, NEG_INF)
        bias = jnp.full((n_blk, blk), NEG_INF, F32)
        for _ in range(MOBA_TOPK):
            top = jnp.max(gate, axis=0, keepdims=True)
            first = jnp.min(jnp.where(gate == top, sub_f, float(n_blk)), axis=0, keepdims=True)
            pick = (sub_f == first) & past
            bias = jnp.where(pick, 0.0, bias)
            gate = jnp.where(pick, NEG_INF, gate)
        for j in range(n_blk):
            bias_ref[hd, j] = bias[j:j + 1, :]
        qs_t.append((q_t * (HEAD_DIM ** -0.5)).astype(BF16))

    def pair_scores(hd, j0):
        off = j0 * blk if isinstance(j0, int) else pl.multiple_of(j0 * blk, 2 * blk)
        s = _dot(k_ref[0, pl.ds(off, 2 * blk), heads[hd]], qs_t[hd])
        return s[:blk] + bias_ref[hd, j0], s[blk:] + bias_ref[hd, j0 + 1]

    s_own = [_dot(k_ref[0, pl.ds(own, blk), lanes], qs_t[hd]) for hd, lanes in enumerate(heads)]
    s_pair = [pair_scores(hd, 0) for hd in range(n_heads)]
    carry = []
    for hd, lanes in enumerate(heads):
        s = jnp.where(key <= qry, s_own[hd], NEG_INF)
        m = jnp.max(s, axis=0, keepdims=True)
        p = jnp.exp(s - m)
        l = jnp.sum(p, axis=0, keepdims=True)
        acc = _dot(vt_ref[0, i, lanes, :], p.astype(BF16))
        carry.append((m, l, acc) + s_pair[hd])

    def body(jj, carry):
        j0 = 2 * jj
        j_next = jnp.minimum(j0 + 2, n_blk - 2)
        s_next = [pair_scores(hd, j_next) for hd in range(n_heads)]
        out = []
        for hd, lanes in enumerate(heads):
            m, l, acc, s0, s1 = carry[hd]
            m_new = jnp.maximum(m, jnp.max(jnp.maximum(s0, s1), axis=0, keepdims=True))
            alpha = jnp.exp(m - m_new)
            p0 = jnp.exp(s0 - m_new)
            p1 = jnp.exp(s1 - m_new)
            l = alpha * l + jnp.sum(p0 + p1, axis=0, keepdims=True)
            v_pair = jnp.concatenate([vt_ref[0, j0, lanes, :], vt_ref[0, j0 + 1, lanes, :]], axis=1)
            p_pair = jnp.concatenate([p0.astype(BF16), p1.astype(BF16)], axis=0)
            acc = alpha * acc + _dot(v_pair, p_pair)
            out.append((m_new, l, acc) + s_next[hd])
        return tuple(out)

    carry = lax.fori_loop(0, (i + 1) // 2, body, tuple(carry))
    for hd, lanes in enumerate(heads):
        m, l, acc = carry[hd][:3]
        o_ref[0, :, lanes] = (acc / l).T.astype(o_ref.dtype)


def _moba_prompt(q, k16, v_t, kmean, n_seq, seq):
    n_blk = seq // MOBA_BLOCK
    width = MOBA_HEADS_PER_STEP * HEAD_DIM
    tile = pl.BlockSpec((1, MOBA_BLOCK, width), lambda n, h, i: (n, i, h))
    return pl.pallas_call(
        _moba_kernel,
        out_shape=jax.ShapeDtypeStruct((n_seq, seq, D_MODEL), BF16),
        grid=(n_seq, N_HEADS // MOBA_HEADS_PER_STEP, n_blk),
        in_specs=[tile,
                  pl.BlockSpec((1, seq, width), lambda n, h, i: (n, 0, h)),
                  pl.BlockSpec((1, n_blk, width, MOBA_BLOCK), lambda n, h, i: (n, 0, h, 0)),
                  pl.BlockSpec((1, n_blk, width), lambda n, h, i: (n, 0, h))],
        out_specs=tile,
        scratch_shapes=[pltpu.VMEM((MOBA_HEADS_PER_STEP, n_blk, 1, MOBA_BLOCK), F32)],
        compiler_params=_cparams("parallel", "parallel", "arbitrary"),
        name="moba_prompt",
    )(q, k16, v_t, kmean)


MEAN_BLOCKS_PER_STEP = 4


def _page_mean_kernel(pt_ref, *refs):
    del pt_ref
    o_ref = refs[-1]
    for b in range(MEAN_BLOCKS_PER_STEP):
        total = jnp.sum(refs[PAGES_PER_BLOCK * b][0], axis=0)
        for t in range(1, PAGES_PER_BLOCK):
            total = total + jnp.sum(refs[PAGES_PER_BLOCK * b + t][0], axis=0)
        o_ref[0, b] = total * (1.0 / MOBA_BLOCK)


def _page_means(cache_k, page_table):
    n_seq, n_pages = page_table.shape
    n_blk = n_pages // PAGES_PER_BLOCK
    per_step = MEAN_BLOCKS_PER_STEP * PAGES_PER_BLOCK
    page = lambda t: pl.BlockSpec(
        (1, PAGE_SIZE, N_HEADS, HEAD_DIM), lambda s, b, pt: (pt[s * n_pages + per_step * b + t], 0, 0, 0))
    out = pl.pallas_call(
        _page_mean_kernel,
        out_shape=jax.ShapeDtypeStruct((n_seq, n_blk, N_HEADS, HEAD_DIM), F32),
        grid_spec=pltpu.PrefetchScalarGridSpec(
            num_scalar_prefetch=1,
            grid=(n_seq, n_blk // MEAN_BLOCKS_PER_STEP),
            in_specs=[page(t) for t in range(per_step)],
            out_specs=pl.BlockSpec((1, MEAN_BLOCKS_PER_STEP, N_HEADS, HEAD_DIM), lambda s, b, pt: (s, b, 0, 0)),
        ),
        compiler_params=_cparams("parallel", "parallel"),
        name="page_means",
    )(page_table.reshape(-1), *([cache_k] * per_step))
    return out.reshape(n_seq, n_blk, D_MODEL)


def _select_kernel(q_ref, mean_ref, idx_ref):
    n_blk = mean_ref.shape[1]
    prod = mean_ref[0] * q_ref[0]
    lane = lax.broadcasted_iota(jnp.int32, (n_blk, LANES), 1)
    sub = lax.broadcasted_iota(jnp.int32, (n_blk, LANES), 0)
    gate = jnp.zeros((n_blk, LANES), F32)
    for hd in range(N_HEADS):
        g_h = jnp.sum(prod[:, hd * HEAD_DIM:(hd + 1) * HEAD_DIM], axis=1, keepdims=True)
        gate = jnp.where(lane == hd, g_h, gate)
    out_sub = lax.broadcasted_iota(jnp.int32, (SUBLANES, LANES), 0)
    out = jnp.zeros((SUBLANES, LANES), jnp.int32)
    for j in range(n_blk):
        g_j = gate[j:j + 1, :]
        beats = (gate > g_j) | ((gate == g_j) & (sub < j))
        rank = jnp.sum(jnp.where(beats, 1.0, 0.0), axis=0, keepdims=True)
        for r in range(MOBA_TOPK):
            out = jnp.where((out_sub == r) & (rank == float(r)), j, out)
    idx_ref[0] = out


def _select_blocks(q, means):
    n_seq, n_blk, _ = means.shape
    idx = pl.pallas_call(
        _select_kernel,
        out_shape=jax.ShapeDtypeStruct((n_seq, SUBLANES, LANES), jnp.int32),
        grid=(n_seq,),
        in_specs=[pl.BlockSpec((1, 1, D_MODEL), lambda s: (s, 0, 0)),
                  pl.BlockSpec((1, n_blk, D_MODEL), lambda s: (s, 0, 0))],
        out_specs=pl.BlockSpec((1, SUBLANES, LANES), lambda s: (s, 0, 0)),
        compiler_params=_cparams("parallel"),
        name="select_blocks",
    )(q.reshape(n_seq, 1, D_MODEL), means)
    return idx[:, :MOBA_TOPK, :N_HEADS]


N_SEL_PAGES = MOBA_TOPK * PAGES_PER_BLOCK


def _decode_attn_kernel(pt_ref, idx_ref, q_ref, kn_ref, vn_ref, k_hbm, v_hbm, o_ref, kbuf, vbuf, sems,
                        *, n_pages):
    s_i = pl.program_id(0)

    def copies(hd, slot):
        r, t = divmod(slot, PAGES_PER_BLOCK)
        blk = idx_ref[(s_i * MOBA_TOPK + r) * N_HEADS + hd]
        pg = pt_ref[s_i * n_pages + PAGES_PER_BLOCK * blk + t]
        return (pltpu.make_async_copy(k_hbm.at[pg, :, hd, :], kbuf.at[hd, slot], sems.at[0, hd, slot]),
                pltpu.make_async_copy(v_hbm.at[pg, :, hd, :], vbuf.at[hd, slot], sems.at[1, hd, slot]))

    for hd in range(N_HEADS):
        for slot in range(N_SEL_PAGES):
            for cp in copies(hd, slot):
                cp.start()

    for hd in range(N_HEADS):
        lanes = slice(hd * HEAD_DIM, (hd + 1) * HEAD_DIM)
        for slot in range(N_SEL_PAGES):
            for cp in copies(hd, slot):
                cp.wait()
        q = q_ref[0, :, lanes] * (HEAD_DIM ** -0.5)
        s_own = jnp.sum(q * kn_ref[0, :, lanes], axis=1, keepdims=True)
        scores = [jnp.sum(kbuf[hd, slot] * q, axis=1, keepdims=True) for slot in range(N_SEL_PAGES)]
        m = s_own
        for s in scores:
            m = jnp.maximum(m, jnp.max(s, axis=0, keepdims=True))
        p_own = jnp.exp(s_own - m)
        l = p_own
        acc = p_own * vn_ref[0, :, lanes]
        for slot, s in enumerate(scores):
            p = jnp.exp(s - m)
            l = l + jnp.sum(p, axis=0, keepdims=True)
            acc = acc + jnp.sum(p * vbuf[hd, slot], axis=0, keepdims=True)
        o_ref[0, :, lanes] = acc / l


def _decode_attn(q, k_new, v_new, cache_k, cache_v, page_table, idx):
    n_seq, n_pages = page_table.shape
    vec = pl.BlockSpec((1, 1, D_MODEL), lambda s, pt, sel: (s, 0, 0))
    hbm = pl.BlockSpec(memory_space=pl.ANY)
    r3 = lambda a: a.reshape(n_seq, 1, D_MODEL)
    slabs = (N_HEADS, N_SEL_PAGES, PAGE_SIZE, HEAD_DIM)
    out = pl.pallas_call(
        functools.partial(_decode_attn_kernel, n_pages=n_pages),
        out_shape=jax.ShapeDtypeStruct((n_seq, 1, D_MODEL), F32),
        grid_spec=pltpu.PrefetchScalarGridSpec(
            num_scalar_prefetch=2,
            grid=(n_seq,),
            in_specs=[vec, vec, vec, hbm, hbm],
            out_specs=vec,
            scratch_shapes=[pltpu.VMEM(slabs, F32), pltpu.VMEM(slabs, F32),
                            pltpu.SemaphoreType.DMA((2, N_HEADS, N_SEL_PAGES))],
        ),
        compiler_params=_cparams("arbitrary"),
        name="decode_attn",
    )(page_table.reshape(-1), idx.reshape(-1), r3(q), r3(k_new), r3(v_new), cache_k, cache_v)
    return out.reshape(n_seq, D_MODEL)


def _cast_kernel(w_ref, o_ref):
    o_ref[...] = w_ref[...].astype(o_ref.dtype)


def _to_bf16(w, layer, row_tile, n_cols=None, col_tile=None):
    _, rows, cols = w.shape
    n_cols = n_cols or cols
    col_tile = col_tile or n_cols
    return pl.pallas_call(
        _cast_kernel,
        out_shape=jax.ShapeDtypeStruct((rows, n_cols), BF16),
        grid=(rows // row_tile, n_cols // col_tile),
        in_specs=[pl.BlockSpec((None, row_tile, col_tile), lambda i, j: (layer, i, j))],
        out_specs=pl.BlockSpec((row_tile, col_tile), lambda i, j: (i, j)),
        compiler_params=_cparams("parallel", "parallel"),
        name="to_bf16",
    )(w)


def _dt_weight(w_in):
    w_dt = w_in[:, D_INNER + SSM_CONV_DIM:].reshape(D_MODEL, SSM_GROUPS, HEADS_PER_GROUP)
    w_dt = jnp.pad(w_dt, ((0, 0), (0, 0), (0, LANES - HEADS_PER_GROUP)))
    return _to_bf16(w_dt.reshape(1, D_MODEL, SSM_GROUPS * LANES), 0, D_MODEL)


W_ROW_TILE = 64
W_DOWN_ROW_TILE = D_FF // 16


def kernel(x_prompt, x_sample, state_conv, state_ssm, cache_k, cache_v, page_table, norm_ffn_a, w_ffn_a_up, w_ffn_a_down, norm_mix, norm_ffn_b, w_ffn_b_up, w_ffn_b_down, ssm_w_in, ssm_conv_w, ssm_conv_b, ssm_dt_bias, ssm_a_log, ssm_d, ssm_norm, ssm_w_out, norm_kv, w_kv, w_q, w_o, norm_final):
    n_p, seq, d = x_prompt.shape
    n_s = x_sample.shape[0]
    past_len = page_table.shape[1] * PAGE_SIZE
    tm_p = 512
    tm_s = n_s

    hp = x_prompt.reshape(n_p * seq, d)
    hs = x_sample.reshape(n_s, d)
    tab_p = _rope_tables(jnp.arange(seq, dtype=jnp.int32))
    tab_s = _rope_tables(jnp.full((n_s,), past_len, jnp.int32))

    conv_p, ssm_p, conv_s, ssm_s = [], [], [], []
    k_p = v_p = k_s = v_s = None
    for layer in range(DEPTH):
        if layer == N_A_LAYERS:
            w_kv16 = _to_bf16(w_kv[None], 0, W_ROW_TILE)
            k_p, v_p, kmean_p, k16, v_t = _kv_proj(hp, norm_kv, w_kv16, tab_p, MOBA_BLOCK, True)
            k_s, v_s = _kv_proj(hs, norm_kv, w_kv16, tab_s, tm_s, False)
            k16 = k16.reshape(n_p, seq, d)
            v_t = v_t.reshape(n_p, seq // MOBA_BLOCK, d, MOBA_BLOCK)
            kmean_p = kmean_p.reshape(n_p, seq // MOBA_BLOCK, d)
            kmean_s = _page_means(cache_k, page_table)

        w_up = _to_bf16(w_ffn_a_up, layer, W_ROW_TILE)
        w_dn = _to_bf16(w_ffn_a_down, layer, W_DOWN_ROW_TILE)
        hp = _ffn(hp, norm_ffn_a[layer], w_up, w_dn, tm_p)
        hs = _ffn(hs, norm_ffn_a[layer], w_up, w_dn, tm_s)

        if layer < N_A_LAYERS:
            w_in = _to_bf16(ssm_w_in, layer, W_ROW_TILE)
            w_dt = _dt_weight(ssm_w_in[layer])
            w_out = _to_bf16(ssm_w_out, layer, W_ROW_TILE)
            ssm_args = (ssm_conv_w[layer], ssm_conv_b[layer], ssm_dt_bias[layer], ssm_a_log[layer], ssm_d[layer])
            z, xbc, dt, tail = _in_proj(hp, norm_mix[layer], w_in, w_dt, ssm_conv_w[layer], ssm_conv_b[layer],
                                        tm_p, seq)
            y, st = _ssd_prompt(xbc, dt, *ssm_args[2:], n_p, seq)
            conv_p.append(tail[:, SUBLANES - (SSM_CONV_W - 1):])
            ssm_p.append(st)
            mix_p = (y.reshape(n_p * seq, D_INNER), w_out, z, ssm_norm[layer])

            z, xbc = _proj(hs, norm_mix[layer], w_in, (D_INNER, SSM_CONV_DIM), tm_s)
            dt, = _proj(hs, norm_mix[layer], w_dt, (SSM_GROUPS * LANES,), tm_s)
            y, st = _ssm_step(xbc, dt, state_conv[layer], state_ssm, layer, *ssm_args)
            conv_s.append(jnp.concatenate([state_conv[layer][:, 1:], xbc[:, None, :]], axis=1))
            ssm_s.append(st)
            mix_s = (y, w_out, z, ssm_norm[layer])
        else:
            j = layer - N_A_LAYERS
            w_q16, w_o16 = _to_bf16(w_q, j, W_ROW_TILE), _to_bf16(w_o, j, W_ROW_TILE)
            q = _q_proj(hp, norm_mix[layer], w_q16, tab_p, tm_p)
            o = _moba_prompt(q.reshape(n_p, seq, d), k16, v_t, kmean_p, n_p, seq)
            mix_p = (o.reshape(n_p * seq, d), w_o16)

            q = _q_proj(hs, norm_mix[layer], w_q16, tab_s, tm_s)
            idx = _select_blocks(q, kmean_s)
            o = _decode_attn(q, k_s, v_s, cache_k, cache_v, page_table, idx)
            mix_s = (o, w_o16)

        w_up = _to_bf16(w_ffn_b_up, layer, W_ROW_TILE)
        w_dn = _to_bf16(w_ffn_b_down, layer, W_DOWN_ROW_TILE)
        final_g = norm_final if layer == DEPTH - 1 else None
        hp = _ffn(hp, norm_ffn_b[layer], w_up, w_dn, tm_p, final_g, mix_p)
        hs = _ffn(hs, norm_ffn_b[layer], w_up, w_dn, tm_s, final_g, mix_s)

    heads = (N_HEADS, HEAD_DIM)
    return (hp.reshape(n_p, seq, d), hs.reshape(n_s, 1, d),
            jnp.stack(conv_p), jnp.stack(ssm_p),
            k_p.reshape(n_p, seq, *heads), v_p.reshape(n_p, seq, *heads),
            jnp.stack(conv_s), jnp.stack(ssm_s),
            k_s.reshape(n_s, 1, *heads), v_s.reshape(n_s, 1, *heads))
```

```python
import functools
import math

import jax
import jax.numpy as jnp
from jax import lax
from jax.experimental import pallas as pl
from jax.experimental.pallas import tpu as pltpu

F32 = jnp.float32
BF16 = jnp.bfloat16

D_MODEL = 1024
DEPTH = 4
N_A_LAYERS = 2
NORM_EPS = 1e-5
D_FF = 2816
D_INNER = 2048
SSM_HEAD_DIM = 64
SSM_HEADS = 32
SSM_GROUPS = 4
HEADS_PER_GROUP = 8
SSM_D_STATE = 128
SSM_CONV_W = 4
SSM_CONV_DIM = D_INNER + 2 * SSM_GROUPS * SSM_D_STATE
SSD_CHUNK = 128
GROUP_X = HEADS_PER_GROUP * SSM_HEAD_DIM
N_HEADS = 8
HEAD_DIM = 128
ROT_DIM = 32
ROPE_THETA = 500000.0
MOBA_BLOCK = 256
MOBA_TOPK = 3
PAGE_SIZE = 128
PAGES_PER_BLOCK = MOBA_BLOCK // PAGE_SIZE

LANES = 128
SUBLANES = 8
VMEM_LIMIT_BYTES = 56 * 1024 * 1024
FF_CHUNK = 256
NEG_INF = float("-inf")


def _cparams(*sem):
    return pltpu.CompilerParams(dimension_semantics=sem, vmem_limit_bytes=VMEM_LIMIT_BYTES)


def _resident(shape):
    nd = len(shape)
    return pl.BlockSpec(shape, lambda *_: (0,) * nd, pipeline_mode=pl.Buffered(1))


def _rms(x, g):
    return x * lax.rsqrt(jnp.mean(x * x, axis=-1, keepdims=True) + NORM_EPS) * g


def _silu(x):
    return x * jax.nn.sigmoid(x)


def _split3(x):
    hi = x.astype(BF16)
    r1 = x - hi.astype(F32)
    mid = r1.astype(BF16)
    lo = (r1 - mid.astype(F32)).astype(BF16)
    return hi, mid, lo


def _dot(a, b):
    return jnp.dot(a, b, preferred_element_type=F32)


def _dot_nt(a, b):
    return lax.dot_general(a, b, (((1,), (1,)), ((), ())), preferred_element_type=F32)


def _ffn_kernel(*refs, mixer, final_norm):
    refs = list(refs)
    o_ref = refs.pop()
    gf_ref = refs.pop() if final_norm else None
    x_ref, g_ref, wg_ref, wu_ref, wdn_ref = refs[:5]
    x = x_ref[...]
    if mixer == "gated":
        y_ref, wo_ref, z_ref, nw_ref = refs[5:]
        y = _rms(y_ref[...].astype(F32) * _silu(z_ref[...].astype(F32)), nw_ref[...])
        x = x + _dot(y.astype(BF16), wo_ref[...])
    elif mixer == "plain":
        y_ref, wo_ref = refs[5:]
        x = x + _dot(y_ref[...].astype(BF16), wo_ref[...])
    xn = _rms(x, g_ref[...]).astype(BF16)
    acc = jnp.zeros_like(x)
    for c in range(D_FF // FF_CHUNK):
        lo, hi = c * FF_CHUNK, (c + 1) * FF_CHUNK
        gate = _dot(xn, wg_ref[:, lo:hi])
        up = _dot(xn, wu_ref[:, lo:hi])
        act = (_silu(gate) * up).astype(BF16)
        acc = acc + _dot(act, wdn_ref[lo:hi, :])
    h = x + 0.5 * acc
    if final_norm:
        h = _rms(h, gf_ref[...])
    o_ref[...] = h


def _ffn(h, g, w_up, w_dn, tm, final_g=None, mix=None):
    t, d = h.shape
    row = pl.BlockSpec((tm, d), lambda i: (i, 0))
    w_gate, w_upper = w_up
    in_specs = [row, _resident((1, d)), _resident(w_gate.shape), _resident(w_upper.shape), _resident(w_dn.shape)]
    args = [h, g.reshape(1, d), w_gate, w_upper, w_dn]
    mixer = None
    if mix is not None:
        y, w_o = mix[:2]
        k = y.shape[1]
        row_k = pl.BlockSpec((tm, k), lambda i: (i, 0))
        in_specs += [row_k, _resident(w_o.shape)]
        args += [y, w_o]
        mixer = "plain"
        if len(mix) == 4:
            in_specs += [row_k, _resident((1, k))]
            args += [mix[2], mix[3].reshape(1, k)]
            mixer = "gated"
    if final_g is not None:
        in_specs.append(_resident((1, d)))
        args.append(final_g.reshape(1, d))
    return pl.pallas_call(
        functools.partial(_ffn_kernel, mixer=mixer, final_norm=final_g is not None),
        out_shape=jax.ShapeDtypeStruct((t, d), F32),
        grid=(t // tm,),
        in_specs=in_specs,
        out_specs=row,
        compiler_params=_cparams("parallel"),
        name="ffn",
    )(*args)


PROJ_CHUNK = 512


def _store_proj(xn, w_ref, o_ref, col0, width):
    for c0 in range(0, width, PROJ_CHUNK):
        c1 = min(c0 + PROJ_CHUNK, width)
        o_ref[:, c0:c1] = _dot(xn, w_ref[:, col0 + c0:col0 + c1]).astype(o_ref.dtype)


def _proj_kernel(x_ref, g_ref, w_ref, *o_refs):
    xn = _rms(x_ref[...], g_ref[...]).astype(BF16)
    col = 0
    for o_ref in o_refs:
        width = o_ref.shape[1]
        _store_proj(xn, w_ref, o_ref, col, width)
        col += width


def _proj(h, g, w, widths, tm):
    t, d = h.shape
    return pl.pallas_call(
        _proj_kernel,
        out_shape=[jax.ShapeDtypeStruct((t, n), F32) for n in widths],
        grid=(t // tm,),
        in_specs=[pl.BlockSpec((tm, d), lambda i: (i, 0)), _resident((1, d)), _resident(w.shape)],
        out_specs=[pl.BlockSpec((tm, n), lambda i: (i, 0)) for n in widths],
        compiler_params=_cparams("parallel"),
        name="proj",
    )(h, g.reshape(1, d), w)


def _rope(y, cos, sin_lo, sin_hi):
    outs = []
    for hd in range(y.shape[1] // HEAD_DIM):
        xh = y[:, hd * HEAD_DIM:(hd + 1) * HEAD_DIM]
        up = pltpu.roll(xh, HEAD_DIM - ROT_DIM // 2, axis=1)
        dn = pltpu.roll(xh, ROT_DIM // 2, axis=1)
        outs.append(xh * cos + up * sin_lo + dn * sin_hi)
    return jnp.concatenate(outs, axis=1)


def _rope_tables(pos):
    half = ROT_DIM // 2
    inv = ROPE_THETA ** (-2.0 * jnp.arange(half, dtype=F32) / ROT_DIM)
    ang = pos.astype(F32)[:, None] * inv[None, :]
    cos, sin = jnp.cos(ang), jnp.sin(ang)
    n = pos.shape[0]
    ones = jnp.ones((n, HEAD_DIM - ROT_DIM), F32)
    zeros = jnp.zeros((n, HEAD_DIM - half), F32)
    cos_t = jnp.concatenate([cos, cos, ones], axis=1)
    sin_lo = jnp.concatenate([-sin, zeros], axis=1)
    sin_hi = jnp.concatenate([jnp.zeros((n, half), F32), sin, jnp.zeros((n, HEAD_DIM - ROT_DIM), F32)], axis=1)
    return cos_t, sin_lo, sin_hi


def _q_kernel(x_ref, g_ref, w_ref, cos_ref, slo_ref, shi_ref, q_ref):
    xn = _rms(x_ref[...], g_ref[...]).astype(BF16)
    q_ref[...] = _rope(_dot(xn, w_ref[...]), cos_ref[...], slo_ref[...], shi_ref[...])


def _q_proj(h, g, w, tables, tm):
    t, d = h.shape
    n_tab = tables[0].shape[0] // tm
    tab = pl.BlockSpec((tm, HEAD_DIM), lambda i: (i % n_tab, 0))
    row = pl.BlockSpec((tm, d), lambda i: (i, 0))
    return pl.pallas_call(
        _q_kernel,
        out_shape=jax.ShapeDtypeStruct((t, d), F32),
        grid=(t // tm,),
        in_specs=[row, _resident((1, d)), _resident(w.shape), tab, tab, tab],
        out_specs=row,
        compiler_params=_cparams("parallel"),
        name="q_proj",
    )(h, g.reshape(1, d), w, *tables)


def _kv_kernel(x_ref, g_ref, w_ref, cos_ref, slo_ref, shi_ref, k_ref, v_ref, *block_refs, block_sums):
    xn = _rms(x_ref[...], g_ref[...]).astype(BF16)
    k = _rope(_dot(xn, w_ref[:, :D_MODEL]), cos_ref[...], slo_ref[...], shi_ref[...])
    v = _dot(xn, w_ref[:, D_MODEL:])
    k_ref[...] = k
    v_ref[...] = v
    if block_sums:
        ksum_ref, k16_ref, vt16_ref = block_refs
        ksum_ref[0] = jnp.sum(k, axis=0, keepdims=True) * (1.0 / MOBA_BLOCK)
        k16_ref[...] = k.astype(BF16)
        vt16_ref[0] = v.T.astype(BF16)


def _kv_proj(h, g, w, tables, tm, block_sums):
    t, d = h.shape
    n_tab = tables[0].shape[0] // tm
    tab = pl.BlockSpec((tm, HEAD_DIM), lambda i: (i % n_tab, 0))
    row = pl.BlockSpec((tm, d), lambda i: (i, 0))
    out_shape = [jax.ShapeDtypeStruct((t, d), F32), jax.ShapeDtypeStruct((t, d), F32)]
    out_specs = [row, row]
    if block_sums:
        assert tm == MOBA_BLOCK
        out_shape += [jax.ShapeDtypeStruct((t // tm, 1, d), F32), jax.ShapeDtypeStruct((t, d), BF16),
                      jax.ShapeDtypeStruct((t // tm, d, tm), BF16)]
        out_specs += [pl.BlockSpec((1, 1, d), lambda i: (i, 0, 0)), row,
                      pl.BlockSpec((1, d, tm), lambda i: (i, 0, 0))]
    return pl.pallas_call(
        functools.partial(_kv_kernel, block_sums=block_sums),
        out_shape=out_shape,
        grid=(t // tm,),
        in_specs=[row, _resident((1, d)), _resident(w.shape), tab, tab, tab],
        out_specs=out_specs,
        compiler_params=_cparams("parallel"),
        name="kv_proj",
    )(h, g.reshape(1, d), w, *tables)


def _in_proj_kernel(x_ref, g_ref, w_ref, wdt_ref, cw_ref, cb_ref, z_ref, xbc_ref, dt_ref, tail_ref, ext_ref,
                    *, tiles_per_seq):
    tm = x_ref.shape[0]
    xn = _rms(x_ref[...], g_ref[...]).astype(BF16)

    @pl.when(pl.program_id(0) % tiles_per_seq == 0)
    def _():
        ext_ref[...] = jnp.zeros((SUBLANES, SSM_CONV_DIM), F32)

    first_rows = lax.broadcasted_iota(jnp.int32, (SUBLANES, PROJ_CHUNK), 0)

    def xbc_chunk(c0):
        return _dot(xn, w_ref[:, D_INNER + c0:D_INNER + c0 + PROJ_CHUNK])

    other = [(z_ref, c0, w_ref) for c0 in range(0, D_INNER, PROJ_CHUNK)]
    other.append((dt_ref, 0, wdt_ref))
    pending = xbc_chunk(0)
    for c0 in range(0, SSM_CONV_DIM, PROJ_CHUNK):
        cols = slice(c0, c0 + PROJ_CHUNK)
        raw = pending
        if c0 + PROJ_CHUNK < SSM_CONV_DIM:
            pending = xbc_chunk(c0 + PROJ_CHUNK)
        if other:
            o_ref, dst, src_ref = other.pop(0)
            o_ref[:, dst:dst + PROJ_CHUNK] = _dot(xn, src_ref[:, dst:dst + PROJ_CHUNK]).astype(o_ref.dtype)
        prev = ext_ref[:, cols]
        acc = cb_ref[:, cols] + cw_ref[SSM_CONV_W - 1:SSM_CONV_W, cols] * raw
        for back in range(1, SSM_CONV_W):
            cur = pltpu.roll(raw, back, axis=0)
            head = jnp.where(first_rows < back, pltpu.roll(prev, back, axis=0), cur[:SUBLANES])
            shifted = jnp.concatenate([head, cur[SUBLANES:]], axis=0)
            acc = acc + cw_ref[SSM_CONV_W - 1 - back:SSM_CONV_W - back, cols] * shifted
        xbc_ref[:, cols] = _silu(acc).astype(xbc_ref.dtype)
        tail = raw[tm - SUBLANES:]
        tail_ref[0, :, cols] = tail
        ext_ref[:, cols] = tail


def _in_proj(h, g, w, w_dt, conv_w, conv_b, tm, seq):
    t, d = h.shape
    tiles_per_seq = seq // tm
    row = lambda n: pl.BlockSpec((tm, n), lambda i: (i, 0))
    return pl.pallas_call(
        functools.partial(_in_proj_kernel, tiles_per_seq=tiles_per_seq),
        out_shape=[jax.ShapeDtypeStruct((t, D_INNER), BF16),
                   jax.ShapeDtypeStruct((t, SSM_CONV_DIM), BF16),
                   jax.ShapeDtypeStruct((t, SSM_GROUPS * LANES), F32),
                   jax.ShapeDtypeStruct((t // seq, SUBLANES, SSM_CONV_DIM), F32)],
        grid=(t // tm,),
        in_specs=[row(d), _resident((1, d)), _resident(w.shape), _resident(w_dt.shape),
                  _resident(conv_w.shape), _resident((1, SSM_CONV_DIM))],
        out_specs=[row(D_INNER), row(SSM_CONV_DIM), row(SSM_GROUPS * LANES),
                   pl.BlockSpec((1, SUBLANES, SSM_CONV_DIM), lambda i: (i // tiles_per_seq, 0, 0))],
        scratch_shapes=[pltpu.VMEM((SUBLANES, SSM_CONV_DIM), F32)],
        compiler_params=_cparams("arbitrary"),
        name="in_proj",
    )(h, g.reshape(1, d), w, w_dt, conv_w, conv_b.reshape(1, SSM_CONV_DIM))


def _split_dot(x, w, terms):
    return _dot(jnp.concatenate(_split3(x)[:terms], axis=1), w)


SSD_GROUPS_PER_STEP = 2


def _ssd_kernel(x_ref, b_ref, c_ref, dt_ref, dtb_ref, alog_ref, dsk_ref, spread_ref, select_ref,
                y_ref, st_ref):
    cs = SSD_CHUNK

    @pl.when(pl.program_id(2) == 0)
    def _():
        st_ref[...] = jnp.zeros(st_ref.shape, F32)

    row = lax.broadcasted_iota(jnp.int32, (cs, cs), 0)
    col = lax.broadcasted_iota(jnp.int32, (cs, cs), 1)
    causal = row >= col
    tril = jnp.where(causal, 1.0, 0.0).astype(BF16)
    low_half = col < SSM_HEAD_DIM
    spread = spread_ref[...]

    for g in range(SSD_GROUPS_PER_STEP):
        heads = slice(g * HEADS_PER_GROUP, (g + 1) * HEADS_PER_GROUP)
        xs = x_ref[0, :, g * GROUP_X:(g + 1) * GROUP_X].astype(F32)
        bm16 = b_ref[0, :, g * SSM_D_STATE:(g + 1) * SSM_D_STATE]
        cm16 = c_ref[0, :, g * SSM_D_STATE:(g + 1) * SSM_D_STATE]

        dt = jax.nn.softplus(dt_ref[:, g * LANES:(g + 1) * LANES] + dtb_ref[g])
        a = jnp.where(col < HEADS_PER_GROUP, dt * (-jnp.exp(alog_ref[g])), 0.0)
        a_hi, a_mid, a_lo = _split3(a)
        acum = _dot(tril, a_hi) + _dot(tril, a_mid) + _dot(tril, a_lo)
        acum_t = acum.T
        ac_last = acum[cs - 1:cs, :]

        xdt = xs * _split_dot(dt, spread, 2)
        xdt16 = xdt.astype(BF16)
        xw = xdt * _split_dot(jnp.exp(ac_last - acum), spread, 2)
        ac_col = _split_dot(acum, select_ref[...], 3)

        cb = _dot_nt(cm16, bm16)
        st_old = st_ref[0, heads].reshape(GROUP_X, SSM_D_STATE)
        y = _dot_nt(cm16, st_old.astype(BF16)) * _split_dot(jnp.exp(acum), spread, 2) + dsk_ref[g] * xs

        for pair in range(HEADS_PER_GROUP // 2):
            x_pair = xdt16[:, pair * LANES:(pair + 1) * LANES]
            y_pair = None
            for half in range(2):
                e = 2 * pair + half
                seg = ac_col[:, e * LANES:(e + 1) * LANES] - acum_t[e:e + 1, :]
                lmat = jnp.exp(jnp.where(causal, seg, NEG_INF))
                x_half = jnp.where(low_half if half == 0 else ~low_half, x_pair, jnp.zeros_like(x_pair))
                part = _dot((cb * lmat).astype(BF16), x_half)
                y_pair = part if y_pair is None else y_pair + part
            lanes = slice(g * GROUP_X + pair * LANES, g * GROUP_X + (pair + 1) * LANES)
            y_ref[0, :, lanes] = (y[:, pair * LANES:(pair + 1) * LANES] + y_pair).astype(y_ref.dtype)

        states = _dot(xw.T.astype(BF16), bm16)
        for e in range(HEADS_PER_GROUP):
            lo, hi = e * SSM_HEAD_DIM, (e + 1) * SSM_HEAD_DIM
            head = g * HEADS_PER_GROUP + e
            st_ref[0, head] = st_ref[0, head] * jnp.exp(ac_last[:, e:e + 1]) + states[lo:hi, :]


def _group_lanes(v):
    v = v.astype(F32).reshape(SSM_GROUPS, 1, HEADS_PER_GROUP)
    return jnp.pad(v, ((0, 0), (0, 0), (0, LANES - HEADS_PER_GROUP)))


def _ssd_prompt(xbc, dt, dt_bias, a_log, d_skip, n_seq, seq):
    nc = seq // SSD_CHUNK
    cs = SSD_CHUNK
    xbc = xbc.reshape(n_seq, seq, SSM_CONV_DIM)
    nb = D_INNER // SSM_D_STATE
    par = _group_lanes
    src = jnp.arange(LANES, dtype=jnp.int32)[:, None]
    spread = (jnp.arange(GROUP_X, dtype=jnp.int32)[None, :] // SSM_HEAD_DIM == src).astype(BF16)
    select = (jnp.arange(HEADS_PER_GROUP * LANES, dtype=jnp.int32)[None, :] // LANES == src).astype(BF16)
    spread2, select3 = jnp.tile(spread, (2, 1)), jnp.tile(select, (3, 1))
    d_chan = jnp.repeat(d_skip.astype(F32), SSM_HEAD_DIM).reshape(SSM_GROUPS, 1, GROUP_X)
    gps = SSD_GROUPS_PER_STEP
    b_blk = nb // gps
    c_blk = (nb + SSM_GROUPS) // gps

    def lanes(width, idx):
        return pl.BlockSpec((1, cs, gps * width), idx)

    def per_group(width):
        return pl.BlockSpec((gps, 1, width), lambda n, g, c: (g, 0, 0))

    in_specs = [
        lanes(GROUP_X, lambda n, g, c: (n, c, g)),
        lanes(SSM_D_STATE, lambda n, g, c: (n, c, b_blk + g)),
        lanes(SSM_D_STATE, lambda n, g, c: (n, c, c_blk + g)),
        pl.BlockSpec((cs, gps * LANES), lambda n, g, c: (n * nc + c, g)),
        per_group(LANES),
        per_group(LANES),
        per_group(GROUP_X),
        _resident(spread2.shape),
        _resident(select3.shape),
    ]
    return pl.pallas_call(
        _ssd_kernel,
        out_shape=[jax.ShapeDtypeStruct((n_seq, seq, D_INNER), BF16),
                   jax.ShapeDtypeStruct((n_seq, SSM_HEADS, SSM_HEAD_DIM, SSM_D_STATE), F32)],
        grid=(n_seq, SSM_GROUPS // gps, nc),
        in_specs=in_specs,
        out_specs=[lanes(GROUP_X, lambda n, g, c: (n, c, g)),
                   pl.BlockSpec((1, gps * HEADS_PER_GROUP, SSM_HEAD_DIM, SSM_D_STATE),
                                lambda n, g, c: (n, g, 0, 0))],
        compiler_params=_cparams("parallel", "parallel", "arbitrary"),
        name="ssd_prompt",
    )(xbc, xbc, xbc, dt, par(dt_bias), par(a_log), d_chan, spread2, select3)


def _ssm_step_kernel(xbc_ref, cst_ref, dt_ref, w_ref, b_ref, dtb_ref, alog_ref, dsk_ref, *refs):
    st_refs, (y_ref, sto_ref) = refs[:SSM_GROUPS], refs[SSM_GROUPS:]
    w = w_ref[...]
    acc = b_ref[...] + w[SSM_CONV_W - 1:SSM_CONV_W, :] * xbc_ref[0]
    for k in range(SSM_CONV_W - 1):
        acc = acc + w[k:k + 1, :] * cst_ref[0, k:k + 1, :]
    act = _silu(acc)
    nb = D_INNER
    row0 = lax.broadcasted_iota(jnp.int32, (SUBLANES, 1), 0) == 0
    hp = lax.Precision.HIGHEST
    for g in range(SSM_GROUPS):
        xs = act[:, g * GROUP_X:(g + 1) * GROUP_X]
        bm = act[:, nb + g * SSM_D_STATE:nb + (g + 1) * SSM_D_STATE]
        cm = act[:, nb + (SSM_GROUPS + g) * SSM_D_STATE:nb + (SSM_GROUPS + g + 1) * SSM_D_STATE]
        dt = jax.nn.softplus(dt_ref[0, :, g * LANES:(g + 1) * LANES] + dtb_ref[g])
        decay = jnp.exp(dt * (-jnp.exp(alog_ref[g])))
        xdt = jnp.concatenate(
            [xs[:, e * SSM_HEAD_DIM:(e + 1) * SSM_HEAD_DIM] * dt[:, e:e + 1] for e in range(HEADS_PER_GROUP)], axis=1)
        x8 = jnp.where(row0, xdt, 0.0)
        b8 = jnp.where(row0, bm, 0.0)
        c8 = jnp.where(row0, cm, 0.0)
        outer = lax.dot_general(x8, b8, (((0,), (0,)), ((), ())), precision=hp,
                                preferred_element_type=F32)
        new = []
        for e in range(HEADS_PER_GROUP):
            head = g * HEADS_PER_GROUP + e
            h_new = st_refs[g][0, e] * decay[:, e:e + 1] + outer[e * SSM_HEAD_DIM:(e + 1) * SSM_HEAD_DIM, :]
            sto_ref[0, head] = h_new
            new.append(h_new)
        h_g = jnp.concatenate(new, axis=0)
        y8 = lax.dot_general(c8, h_g, (((1,), (1,)), ((), ())), precision=hp,
                             preferred_element_type=F32)
        dsk = dsk_ref[g]
        dx = jnp.concatenate(
            [xs[:, e * SSM_HEAD_DIM:(e + 1) * SSM_HEAD_DIM] * dsk[:, e:e + 1] for e in range(HEADS_PER_GROUP)], axis=1)
        y_ref[0, :, g * GROUP_X:(g + 1) * GROUP_X] = y8[0:1, :] + dx


def _ssm_step(xbc, dt, conv_state, ssm_states, layer, conv_w, conv_b, dt_bias, a_log, d_skip):
    n = xbc.shape[0]
    par = _group_lanes
    st_spec = pl.BlockSpec((1, SSM_HEADS, SSM_HEAD_DIM, SSM_D_STATE), lambda s: (s, 0, 0, 0))
    st_in_specs = [pl.BlockSpec((None, 1, HEADS_PER_GROUP, SSM_HEAD_DIM, SSM_D_STATE),
                                lambda s, g=g: (layer, s, g, 0, 0)) for g in range(SSM_GROUPS)]
    y, st = pl.pallas_call(
        _ssm_step_kernel,
        out_shape=[jax.ShapeDtypeStruct((n, 1, D_INNER), F32),
                   jax.ShapeDtypeStruct(ssm_states.shape[1:], F32)],
        grid=(n,),
        in_specs=[pl.BlockSpec((1, 1, SSM_CONV_DIM), lambda s: (s, 0, 0)),
                  pl.BlockSpec((1, SSM_CONV_W - 1, SSM_CONV_DIM), lambda s: (s, 0, 0)),
                  pl.BlockSpec((1, 1, SSM_GROUPS * LANES), lambda s: (s, 0, 0)),
                  _resident((SSM_CONV_W, SSM_CONV_DIM)),
                  _resident((1, SSM_CONV_DIM)),
                  _resident((SSM_GROUPS, 1, LANES)),
                  _resident((SSM_GROUPS, 1, LANES)),
                  _resident((SSM_GROUPS, 1, LANES)),
                  *st_in_specs],
        out_specs=[pl.BlockSpec((1, 1, D_INNER), lambda s: (s, 0, 0)), st_spec],
        compiler_params=_cparams("parallel"),
        name="ssm_step",
    )(xbc.reshape(n, 1, SSM_CONV_DIM), conv_state, dt.reshape(n, 1, SSM_GROUPS * LANES), conv_w,
      conv_b.reshape(1, SSM_CONV_DIM), par(dt_bias), par(a_log), par(d_skip), *([ssm_states] * SSM_GROUPS))
    return y.reshape(n, D_INNER), st


MOBA_HEADS_PER_STEP = 4


def _moba_kernel(q_ref, k_ref, vt_ref, mean_ref, o_ref, bias_ref):
    i = pl.program_id(2)
    blk = MOBA_BLOCK
    n_blk = mean_ref.shape[1]
    n_heads = q_ref.shape[2] // HEAD_DIM
    sub = lax.broadcasted_iota(jnp.int32, (n_blk, blk), 0)
    sub_f = sub.astype(F32)
    key = lax.broadcasted_iota(jnp.int32, (blk, blk), 0)
    qry = lax.broadcasted_iota(jnp.int32, (blk, blk), 1)
    own = pl.multiple_of(i * blk, blk)
    heads = [slice(hd * HEAD_DIM, (hd + 1) * HEAD_DIM) for hd in range(n_heads)]

    qs_t = []
    for hd, lanes in enumerate(heads):
        q_t = q_ref[0, :, lanes].T
        gate = jnp.dot(mean_ref[0, :, lanes], q_t, precision=lax.Precision.HIGHEST,
                       preferred_element_type=F32)
        past = sub < i
        gate = jnp.where(past, gate, NEG_INF)
        bias = jnp.full((n_blk, blk), NEG_INF, F32)
        for _ in range(MOBA_TOPK):
            top = jnp.max(gate, axis=0, keepdims=True)
            first = jnp.min(jnp.where(gate == top, sub_f, float(n_blk)), axis=0, keepdims=True)
            pick = (sub_f == first) & past
            bias = jnp.where(pick, 0.0, bias)
            gate = jnp.where(pick, NEG_INF, gate)
        for j in range(n_blk):
            bias_ref[hd, j] = bias[j:j + 1, :]
        qs_t.append((q_t * (HEAD_DIM ** -0.5)).astype(BF16))

    def pair_scores(hd, j0):
        off = j0 * blk if isinstance(j0, int) else pl.multiple_of(j0 * blk, 2 * blk)
        s = _dot(k_ref[0, pl.ds(off, 2 * blk), heads[hd]], qs_t[hd])
        return s[:blk] + bias_ref[hd, j0], s[blk:] + bias_ref[hd, j0 + 1]

    s_own = [_dot(k_ref[0, pl.ds(own, blk), lanes], qs_t[hd]) for hd, lanes in enumerate(heads)]
    s_pair = [pair_scores(hd, 0) for hd in range(n_heads)]
    carry = []
    for hd, lanes in enumerate(heads):
        s = jnp.where(key <= qry, s_own[hd], NEG_INF)
        m = jnp.max(s, axis=0, keepdims=True)
        p = jnp.exp(s - m)
        l = jnp.sum(p, axis=0, keepdims=True)
        acc = _dot(vt_ref[0, i, lanes, :], p.astype(BF16))
        carry.append((m, l, acc) + s_pair[hd])

    def body(jj, carry):
        j0 = 2 * jj
        j_next = jnp.minimum(j0 + 2, n_blk - 2)
        s_next = [pair_scores(hd, j_next) for hd in range(n_heads)]
        out = []
        for hd, lanes in enumerate(heads):
            m, l, acc, s0, s1 = carry[hd]
            m_new = jnp.maximum(m, jnp.max(jnp.maximum(s0, s1), axis=0, keepdims=True))
            alpha = jnp.exp(m - m_new)
            p0 = jnp.exp(s0 - m_new)
            p1 = jnp.exp(s1 - m_new)
            l = alpha * l + jnp.sum(p0 + p1, axis=0, keepdims=True)
            v_pair = jnp.concatenate([vt_ref[0, j0, lanes, :], vt_ref[0, j0 + 1, lanes, :]], axis=1)
            p_pair = jnp.concatenate([p0.astype(BF16), p1.astype(BF16)], axis=0)
            acc = alpha * acc + _dot(v_pair, p_pair)
            out.append((m_new, l, acc) + s_next[hd])
        return tuple(out)

    carry = lax.fori_loop(0, (i + 1) // 2, body, tuple(carry))
    for hd, lanes in enumerate(heads):
        m, l, acc = carry[hd][:3]
        o_ref[0, :, lanes] = (acc / l).T.astype(o_ref.dtype)


def _moba_prompt(q, k16, v_t, kmean, n_seq, seq):
    n_blk = seq // MOBA_BLOCK
    width = MOBA_HEADS_PER_STEP * HEAD_DIM
    tile = pl.BlockSpec((1, MOBA_BLOCK, width), lambda n, h, i: (n, i, h))
    return pl.pallas_call(
        _moba_kernel,
        out_shape=jax.ShapeDtypeStruct((n_seq, seq, D_MODEL), BF16),
        grid=(n_seq, N_HEADS // MOBA_HEADS_PER_STEP, n_blk),
        in_specs=[tile,
                  pl.BlockSpec((1, seq, width), lambda n, h, i: (n, 0, h)),
                  pl.BlockSpec((1, n_blk, width, MOBA_BLOCK), lambda n, h, i: (n, 0, h, 0)),
                  pl.BlockSpec((1, n_blk, width), lambda n, h, i: (n, 0, h))],
        out_specs=tile,
        scratch_shapes=[pltpu.VMEM((MOBA_HEADS_PER_STEP, n_blk, 1, MOBA_BLOCK), F32)],
        compiler_params=_cparams("parallel", "parallel", "arbitrary"),
        name="moba_prompt",
    )(q, k16, v_t, kmean)


MEAN_BLOCKS_PER_STEP = 4


def _page_mean_kernel(pt_ref, *refs):
    del pt_ref
    o_ref = refs[-1]
    for b in range(MEAN_BLOCKS_PER_STEP):
        total = jnp.sum(refs[PAGES_PER_BLOCK * b][0], axis=0)
        for t in range(1, PAGES_PER_BLOCK):
            total = total + jnp.sum(refs[PAGES_PER_BLOCK * b + t][0], axis=0)
        o_ref[0, b] = total * (1.0 / MOBA_BLOCK)


def _page_means(cache_k, page_table):
    n_seq, n_pages = page_table.shape
    n_blk = n_pages // PAGES_PER_BLOCK
    per_step = MEAN_BLOCKS_PER_STEP * PAGES_PER_BLOCK
    page = lambda t: pl.BlockSpec(
        (1, PAGE_SIZE, N_HEADS, HEAD_DIM), lambda s, b, pt: (pt[s * n_pages + per_step * b + t], 0, 0, 0))
    out = pl.pallas_call(
        _page_mean_kernel,
        out_shape=jax.ShapeDtypeStruct((n_seq, n_blk, N_HEADS, HEAD_DIM), F32),
        grid_spec=pltpu.PrefetchScalarGridSpec(
            num_scalar_prefetch=1,
            grid=(n_seq, n_blk // MEAN_BLOCKS_PER_STEP),
            in_specs=[page(t) for t in range(per_step)],
            out_specs=pl.BlockSpec((1, MEAN_BLOCKS_PER_STEP, N_HEADS, HEAD_DIM), lambda s, b, pt: (s, b, 0, 0)),
        ),
        compiler_params=_cparams("parallel", "parallel"),
        name="page_means",
    )(page_table.reshape(-1), *([cache_k] * per_step))
    return out.reshape(n_seq, n_blk, D_MODEL)


def _select_kernel(q_ref, mean_ref, idx_ref):
    n_blk = mean_ref.shape[1]
    prod = mean_ref[0] * q_ref[0]
    lane = lax.broadcasted_iota(jnp.int32, (n_blk, LANES), 1)
    sub = lax.broadcasted_iota(jnp.int32, (n_blk, LANES), 0)
    gate = jnp.zeros((n_blk, LANES), F32)
    for hd in range(N_HEADS):
        g_h = jnp.sum(prod[:, hd * HEAD_DIM:(hd + 1) * HEAD_DIM], axis=1, keepdims=True)
        gate = jnp.where(lane == hd, g_h, gate)
    out_sub = lax.broadcasted_iota(jnp.int32, (SUBLANES, LANES), 0)
    out = jnp.zeros((SUBLANES, LANES), jnp.int32)
    for j in range(n_blk):
        g_j = gate[j:j + 1, :]
        beats = (gate > g_j) | ((gate == g_j) & (sub < j))
        rank = jnp.sum(jnp.where(beats, 1.0, 0.0), axis=0, keepdims=True)
        for r in range(MOBA_TOPK):
            out = jnp.where((out_sub == r) & (rank == float(r)), j, out)
    idx_ref[0] = out


def _select_blocks(q, means):
    n_seq, n_blk, _ = means.shape
    idx = pl.pallas_call(
        _select_kernel,
        out_shape=jax.ShapeDtypeStruct((n_seq, SUBLANES, LANES), jnp.int32),
        grid=(n_seq,),
        in_specs=[pl.BlockSpec((1, 1, D_MODEL), lambda s: (s, 0, 0)),
                  pl.BlockSpec((1, n_blk, D_MODEL), lambda s: (s, 0, 0))],
        out_specs=pl.BlockSpec((1, SUBLANES, LANES), lambda s: (s, 0, 0)),
        compiler_params=_cparams("parallel"),
        name="select_blocks",
    )(q.reshape(n_seq, 1, D_MODEL), means)
    return idx[:, :MOBA_TOPK, :N_HEADS]


N_SEL_PAGES = MOBA_TOPK * PAGES_PER_BLOCK


def _decode_attn_kernel(pt_ref, idx_ref, q_ref, kn_ref, vn_ref, k_hbm, v_hbm, o_ref, kbuf, vbuf, sems,
                        *, n_pages):
    s_i = pl.program_id(0)
    n_seq = pl.num_programs(0)
    buf = s_i % 2

    def copies(seq, hd, slot):
        r, t = divmod(slot, PAGES_PER_BLOCK)
        blk = idx_ref[(seq * MOBA_TOPK + r) * N_HEADS + hd]
        pg = pt_ref[seq * n_pages + PAGES_PER_BLOCK * blk + t]
        half = seq % 2
        return (pltpu.make_async_copy(k_hbm.at[pg, :, hd, :], kbuf.at[half, hd, slot], sems.at[half, 0, hd, slot]),
                pltpu.make_async_copy(v_hbm.at[pg, :, hd, :], vbuf.at[half, hd, slot], sems.at[half, 1, hd, slot]))

    def start_all(seq):
        for hd in range(N_HEADS):
            for slot in range(N_SEL_PAGES):
                for cp in copies(seq, hd, slot):
                    cp.start()

    @pl.when(s_i == 0)
    def _():
        start_all(s_i)

    @pl.when(s_i + 1 < n_seq)
    def _():
        start_all(s_i + 1)

    for hd in range(N_HEADS):
        lanes = slice(hd * HEAD_DIM, (hd + 1) * HEAD_DIM)
        for slot in range(N_SEL_PAGES):
            for cp in copies(s_i, hd, slot):
                cp.wait()
        q = q_ref[0, :, lanes] * (HEAD_DIM ** -0.5)
        s_own = jnp.sum(q * kn_ref[0, :, lanes], axis=1, keepdims=True)
        scores = [jnp.sum(kbuf[buf, hd, slot] * q, axis=1, keepdims=True) for slot in range(N_SEL_PAGES)]
        m = s_own
        for s in scores:
            m = jnp.maximum(m, jnp.max(s, axis=0, keepdims=True))
        p_own = jnp.exp(s_own - m)
        l = p_own
        acc = p_own * vn_ref[0, :, lanes]
        for slot, s in enumerate(scores):
            p = jnp.exp(s - m)
            l = l + jnp.sum(p, axis=0, keepdims=True)
            acc = acc + jnp.sum(p * vbuf[buf, hd, slot], axis=0, keepdims=True)
        o_ref[0, :, lanes] = acc / l


def _decode_attn(q, k_new, v_new, cache_k, cache_v, page_table, idx):
    n_seq, n_pages = page_table.shape
    vec = pl.BlockSpec((1, 1, D_MODEL), lambda s, pt, sel: (s, 0, 0))
    hbm = pl.BlockSpec(memory_space=pl.ANY)
    r3 = lambda a: a.reshape(n_seq, 1, D_MODEL)
    slabs = (2, N_HEADS, N_SEL_PAGES, PAGE_SIZE, HEAD_DIM)
    out = pl.pallas_call(
        functools.partial(_decode_attn_kernel, n_pages=n_pages),
        out_shape=jax.ShapeDtypeStruct((n_seq, 1, D_MODEL), F32),
        grid_spec=pltpu.PrefetchScalarGridSpec(
            num_scalar_prefetch=2,
            grid=(n_seq,),
            in_specs=[vec, vec, vec, hbm, hbm],
            out_specs=vec,
            scratch_shapes=[pltpu.VMEM(slabs, F32), pltpu.VMEM(slabs, F32),
                            pltpu.SemaphoreType.DMA((2, 2, N_HEADS, N_SEL_PAGES))],
        ),
        compiler_params=_cparams("arbitrary"),
        name="decode_attn",
    )(page_table.reshape(-1), idx.reshape(-1), r3(q), r3(k_new), r3(v_new), cache_k, cache_v)
    return out.reshape(n_seq, D_MODEL)


def _cast_kernel(*refs, n_parts):
    w_refs, o_refs = refs[:-n_parts], refs[-n_parts:]
    streams = len(w_refs) // n_parts
    rows = w_refs[0].shape[0]
    for p, o_ref in enumerate(o_refs):
        for r in range(streams):
            o_ref[r * rows:(r + 1) * rows, :] = w_refs[p * streams + r][...].astype(o_ref.dtype)


CAST_STREAMS = 4


def _to_bf16(w, layer, row_tile, col_parts=1):
    _, rows, cols = w.shape
    streams = CAST_STREAMS if rows % (CAST_STREAMS * row_tile) == 0 else 1
    width = cols // col_parts
    in_specs = [pl.BlockSpec((None, row_tile, width), lambda i, p=p, r=r: (layer, i * streams + r, p))
                for p in range(col_parts) for r in range(streams)]
    out = pl.pallas_call(
        functools.partial(_cast_kernel, n_parts=col_parts),
        out_shape=[jax.ShapeDtypeStruct((rows, width), BF16)] * col_parts,
        grid=(rows // (streams * row_tile),),
        in_specs=in_specs,
        out_specs=[pl.BlockSpec((streams * row_tile, width), lambda i: (i, 0))] * col_parts,
        compiler_params=_cparams("parallel"),
        name="to_bf16",
    )(*([w] * len(in_specs)))
    return out[0] if col_parts == 1 else tuple(out)


def _dt_weight(w_in):
    w_dt = w_in[:, D_INNER + SSM_CONV_DIM:].reshape(D_MODEL, SSM_GROUPS, HEADS_PER_GROUP)
    w_dt = jnp.pad(w_dt, ((0, 0), (0, 0), (0, LANES - HEADS_PER_GROUP)))
    return _to_bf16(w_dt.reshape(1, D_MODEL, SSM_GROUPS * LANES), 0, D_MODEL)


W_ROW_TILE = 32
W_DOWN_ROW_TILE = D_FF // 16


def kernel(x_prompt, x_sample, state_conv, state_ssm, cache_k, cache_v, page_table, norm_ffn_a, w_ffn_a_up, w_ffn_a_down, norm_mix, norm_ffn_b, w_ffn_b_up, w_ffn_b_down, ssm_w_in, ssm_conv_w, ssm_conv_b, ssm_dt_bias, ssm_a_log, ssm_d, ssm_norm, ssm_w_out, norm_kv, w_kv, w_q, w_o, norm_final):
    n_p, seq, d = x_prompt.shape
    n_s = x_sample.shape[0]
    past_len = page_table.shape[1] * PAGE_SIZE
    tm_p = 512
    tm_s = n_s

    hp = x_prompt.reshape(n_p * seq, d)
    hs = x_sample.reshape(n_s, d)
    tab_p = _rope_tables(jnp.arange(seq, dtype=jnp.int32))
    tab_s = _rope_tables(jnp.full((n_s,), past_len, jnp.int32))

    conv_p, ssm_p, conv_s, ssm_s = [], [], [], []
    k_p = v_p = k_s = v_s = None
    for layer in range(DEPTH):
        if layer == N_A_LAYERS:
            w_kv16 = _to_bf16(w_kv[None], 0, W_ROW_TILE)
            k_p, v_p, kmean_p, k16, v_t = _kv_proj(hp, norm_kv, w_kv16, tab_p, MOBA_BLOCK, True)
            k_s, v_s = _kv_proj(hs, norm_kv, w_kv16, tab_s, tm_s, False)
            k16 = k16.reshape(n_p, seq, d)
            v_t = v_t.reshape(n_p, seq // MOBA_BLOCK, d, MOBA_BLOCK)
            kmean_p = kmean_p.reshape(n_p, seq // MOBA_BLOCK, d)
            kmean_s = _page_means(cache_k, page_table)

        w_up = _to_bf16(w_ffn_a_up, layer, W_ROW_TILE, 2)
        w_dn = _to_bf16(w_ffn_a_down, layer, W_DOWN_ROW_TILE)
        hp = _ffn(hp, norm_ffn_a[layer], w_up, w_dn, tm_p)
        hs = _ffn(hs, norm_ffn_a[layer], w_up, w_dn, tm_s)

        if layer < N_A_LAYERS:
            w_in = _to_bf16(ssm_w_in, layer, W_ROW_TILE)
            w_dt = _dt_weight(ssm_w_in[layer])
            w_out = _to_bf16(ssm_w_out, layer, W_ROW_TILE)
            ssm_args = (ssm_conv_w[layer], ssm_conv_b[layer], ssm_dt_bias[layer], ssm_a_log[layer], ssm_d[layer])
            z, xbc, dt, tail = _in_proj(hp, norm_mix[layer], w_in, w_dt, ssm_conv_w[layer], ssm_conv_b[layer],
                                        tm_p, seq)
            y, st = _ssd_prompt(xbc, dt, *ssm_args[2:], n_p, seq)
            conv_p.append(tail[:, SUBLANES - (SSM_CONV_W - 1):])
            ssm_p.append(st)
            mix_p = (y.reshape(n_p * seq, D_INNER), w_out, z, ssm_norm[layer])

            z, xbc = _proj(hs, norm_mix[layer], w_in, (D_INNER, SSM_CONV_DIM), tm_s)
            dt, = _proj(hs, norm_mix[layer], w_dt, (SSM_GROUPS * LANES,), tm_s)
            y, st = _ssm_step(xbc, dt, state_conv[layer], state_ssm, layer, *ssm_args)
            conv_s.append(jnp.concatenate([state_conv[layer][:, 1:], xbc[:, None, :]], axis=1))
            ssm_s.append(st)
            mix_s = (y, w_out, z, ssm_norm[layer])
        else:
            j = layer - N_A_LAYERS
            w_q16, w_o16 = _to_bf16(w_q, j, W_ROW_TILE), _to_bf16(w_o, j, W_ROW_TILE)
            q = _q_proj(hp, norm_mix[layer], w_q16, tab_p, tm_p)
            o = _moba_prompt(q.reshape(n_p, seq, d), k16, v_t, kmean_p, n_p, seq)
            mix_p = (o.reshape(n_p * seq, d), w_o16)

            q = _q_proj(hs, norm_mix[layer], w_q16, tab_s, tm_s)
            idx = _select_blocks(q, kmean_s)
            o = _decode_attn(q, k_s, v_s, cache_k, cache_v, page_table, idx)
            mix_s = (o, w_o16)

        w_up = _to_bf16(w_ffn_b_up, layer, W_ROW_TILE, 2)
        w_dn = _to_bf16(w_ffn_b_down, layer, W_DOWN_ROW_TILE)
        final_g = norm_final if layer == DEPTH - 1 else None
        hp = _ffn(hp, norm_ffn_b[layer], w_up, w_dn, tm_p, final_g, mix_p)
        hs = _ffn(hs, norm_ffn_b[layer], w_up, w_dn, tm_s, final_g, mix_s)

    heads = (N_HEADS, HEAD_DIM)
    return (hp.reshape(n_p, seq, d), hs.reshape(n_s, 1, d),
            jnp.stack(conv_p), jnp.stack(ssm_p),
            k_p.reshape(n_p, seq, *heads), v_p.reshape(n_p, seq, *heads),
            jnp.stack(conv_s), jnp.stack(ssm_s),
            k_s.reshape(n_s, 1, *heads), v_s.reshape(n_s, 1, *heads))
```

```python
import functools
import math

import jax
import jax.numpy as jnp
from jax import lax
from jax.experimental import pallas as pl
from jax.experimental.pallas import tpu as pltpu

F32 = jnp.float32
BF16 = jnp.bfloat16

D_MODEL = 1024
DEPTH = 4
N_A_LAYERS = 2
NORM_EPS = 1e-5
D_FF = 2816
D_INNER = 2048
SSM_HEAD_DIM = 64
SSM_HEADS = 32
SSM_GROUPS = 4
HEADS_PER_GROUP = 8
SSM_D_STATE = 128
SSM_CONV_W = 4
SSM_CONV_DIM = D_INNER + 2 * SSM_GROUPS * SSM_D_STATE
SSD_CHUNK = 128
GROUP_X = HEADS_PER_GROUP * SSM_HEAD_DIM
N_HEADS = 8
HEAD_DIM = 128
ROT_DIM = 32
ROPE_THETA = 500000.0
MOBA_BLOCK = 256
MOBA_TOPK = 3
PAGE_SIZE = 128
PAGES_PER_BLOCK = MOBA_BLOCK // PAGE_SIZE

LANES = 128
SUBLANES = 8
VMEM_LIMIT_BYTES = 56 * 1024 * 1024
FF_CHUNK = 256
NEG_INF = float("-inf")


def _cparams(*sem):
    return pltpu.CompilerParams(dimension_semantics=sem, vmem_limit_bytes=VMEM_LIMIT_BYTES)


def _resident(shape):
    nd = len(shape)
    return pl.BlockSpec(shape, lambda *_: (0,) * nd, pipeline_mode=pl.Buffered(1))


def _rms(x, g):
    return x * lax.rsqrt(jnp.mean(x * x, axis=-1, keepdims=True) + NORM_EPS) * g


def _silu(x):
    return x * jax.nn.sigmoid(x)


def _split3(x):
    hi = x.astype(BF16)
    r1 = x - hi.astype(F32)
    mid = r1.astype(BF16)
    lo = (r1 - mid.astype(F32)).astype(BF16)
    return hi, mid, lo


def _dot(a, b):
    return jnp.dot(a, b, preferred_element_type=F32)


def _dot_nt(a, b):
    return lax.dot_general(a, b, (((1,), (1,)), ((), ())), preferred_element_type=F32)


def _ffn_kernel(*refs, mixer, final_norm):
    refs = list(refs)
    o_ref = refs.pop()
    gf_ref = refs.pop() if final_norm else None
    x_ref, g_ref, wg_ref, wu_ref, wdn_ref = refs[:5]
    x = x_ref[...]
    if mixer == "gated":
        y_ref, wo_ref, z_ref, nw_ref = refs[5:]
        y = _rms(y_ref[...].astype(F32) * _silu(z_ref[...].astype(F32)), nw_ref[...])
        x = x + _dot(y.astype(BF16), wo_ref[...])
    elif mixer == "plain":
        y_ref, wo_ref = refs[5:]
        x = x + _dot(y_ref[...].astype(BF16), wo_ref[...])
    xn = _rms(x, g_ref[...]).astype(BF16)
    acc = jnp.zeros_like(x)
    for c in range(D_FF // FF_CHUNK):
        lo, hi = c * FF_CHUNK, (c + 1) * FF_CHUNK
        gate = _dot(xn, wg_ref[:, lo:hi])
        up = _dot(xn, wu_ref[:, lo:hi])
        act = (_silu(gate) * up).astype(BF16)
        acc = acc + _dot(act, wdn_ref[lo:hi, :])
    h = x + 0.5 * acc
    if final_norm:
        h = _rms(h, gf_ref[...])
    o_ref[...] = h


def _ffn(h, g, w_up, w_dn, tm, final_g=None, mix=None):
    t, d = h.shape
    row = pl.BlockSpec((tm, d), lambda i: (i, 0))
    w_gate, w_upper = w_up
    in_specs = [row, _resident((1, d)), _resident(w_gate.shape), _resident(w_upper.shape), _resident(w_dn.shape)]
    args = [h, g.reshape(1, d), w_gate, w_upper, w_dn]
    mixer = None
    if mix is not None:
        y, w_o = mix[:2]
        k = y.shape[1]
        row_k = pl.BlockSpec((tm, k), lambda i: (i, 0))
        in_specs += [row_k, _resident(w_o.shape)]
        args += [y, w_o]
        mixer = "plain"
        if len(mix) == 4:
            in_specs += [row_k, _resident((1, k))]
            args += [mix[2], mix[3].reshape(1, k)]
            mixer = "gated"
    if final_g is not None:
        in_specs.append(_resident((1, d)))
        args.append(final_g.reshape(1, d))
    return pl.pallas_call(
        functools.partial(_ffn_kernel, mixer=mixer, final_norm=final_g is not None),
        out_shape=jax.ShapeDtypeStruct((t, d), F32),
        grid=(t // tm,),
        in_specs=in_specs,
        out_specs=row,
        compiler_params=_cparams("parallel"),
        name="ffn",
    )(*args)


PROJ_CHUNK = 512


def _store_proj(xn, w_ref, o_ref, col0, width):
    for c0 in range(0, width, PROJ_CHUNK):
        c1 = min(c0 + PROJ_CHUNK, width)
        o_ref[:, c0:c1] = _dot(xn, w_ref[:, col0 + c0:col0 + c1]).astype(o_ref.dtype)


def _proj_kernel(x_ref, g_ref, w_ref, *o_refs):
    xn = _rms(x_ref[...], g_ref[...]).astype(BF16)
    col = 0
    for o_ref in o_refs:
        width = o_ref.shape[1]
        _store_proj(xn, w_ref, o_ref, col, width)
        col += width


def _proj(h, g, w, widths, tm):
    t, d = h.shape
    return pl.pallas_call(
        _proj_kernel,
        out_shape=[jax.ShapeDtypeStruct((t, n), F32) for n in widths],
        grid=(t // tm,),
        in_specs=[pl.BlockSpec((tm, d), lambda i: (i, 0)), _resident((1, d)), _resident(w.shape)],
        out_specs=[pl.BlockSpec((tm, n), lambda i: (i, 0)) for n in widths],
        compiler_params=_cparams("parallel"),
        name="proj",
    )(h, g.reshape(1, d), w)


def _rope(y, cos, sin_lo, sin_hi):
    outs = []
    for hd in range(y.shape[1] // HEAD_DIM):
        xh = y[:, hd * HEAD_DIM:(hd + 1) * HEAD_DIM]
        up = pltpu.roll(xh, HEAD_DIM - ROT_DIM // 2, axis=1)
        dn = pltpu.roll(xh, ROT_DIM // 2, axis=1)
        outs.append(xh * cos + up * sin_lo + dn * sin_hi)
    return jnp.concatenate(outs, axis=1)


def _rope_tables(pos):
    half = ROT_DIM // 2
    inv = ROPE_THETA ** (-2.0 * jnp.arange(half, dtype=F32) / ROT_DIM)
    ang = pos.astype(F32)[:, None] * inv[None, :]
    cos, sin = jnp.cos(ang), jnp.sin(ang)
    n = pos.shape[0]
    ones = jnp.ones((n, HEAD_DIM - ROT_DIM), F32)
    zeros = jnp.zeros((n, HEAD_DIM - half), F32)
    cos_t = jnp.concatenate([cos, cos, ones], axis=1)
    sin_lo = jnp.concatenate([-sin, zeros], axis=1)
    sin_hi = jnp.concatenate([jnp.zeros((n, half), F32), sin, jnp.zeros((n, HEAD_DIM - ROT_DIM), F32)], axis=1)
    return cos_t, sin_lo, sin_hi


def _q_kernel(x_ref, g_ref, w_ref, cos_ref, slo_ref, shi_ref, q_ref):
    xn = _rms(x_ref[...], g_ref[...]).astype(BF16)
    q_ref[...] = _rope(_dot(xn, w_ref[...]), cos_ref[...], slo_ref[...], shi_ref[...])


def _q_proj(h, g, w, tables, tm):
    t, d = h.shape
    n_tab = tables[0].shape[0] // tm
    tab = pl.BlockSpec((tm, HEAD_DIM), lambda i: (i % n_tab, 0))
    row = pl.BlockSpec((tm, d), lambda i: (i, 0))
    return pl.pallas_call(
        _q_kernel,
        out_shape=jax.ShapeDtypeStruct((t, d), F32),
        grid=(t // tm,),
        in_specs=[row, _resident((1, d)), _resident(w.shape), tab, tab, tab],
        out_specs=row,
        compiler_params=_cparams("parallel"),
        name="q_proj",
    )(h, g.reshape(1, d), w, *tables)


def _kv_kernel(x_ref, g_ref, w_ref, cos_ref, slo_ref, shi_ref, k_ref, v_ref, *block_refs, block_sums):
    xn = _rms(x_ref[...], g_ref[...]).astype(BF16)
    k = _rope(_dot(xn, w_ref[:, :D_MODEL]), cos_ref[...], slo_ref[...], shi_ref[...])
    v = _dot(xn, w_ref[:, D_MODEL:])
    k_ref[...] = k
    v_ref[...] = v
    if block_sums:
        ksum_ref, k16_ref, vt16_ref = block_refs
        ksum_ref[0] = jnp.sum(k, axis=0, keepdims=True) * (1.0 / MOBA_BLOCK)
        k16_ref[...] = k.astype(BF16)
        vt16_ref[0] = v.T.astype(BF16)


def _kv_proj(h, g, w, tables, tm, block_sums):
    t, d = h.shape
    n_tab = tables[0].shape[0] // tm
    tab = pl.BlockSpec((tm, HEAD_DIM), lambda i: (i % n_tab, 0))
    row = pl.BlockSpec((tm, d), lambda i: (i, 0))
    out_shape = [jax.ShapeDtypeStruct((t, d), F32), jax.ShapeDtypeStruct((t, d), F32)]
    out_specs = [row, row]
    if block_sums:
        assert tm == MOBA_BLOCK
        out_shape += [jax.ShapeDtypeStruct((t // tm, 1, d), F32), jax.ShapeDtypeStruct((t, d), BF16),
                      jax.ShapeDtypeStruct((t // tm, d, tm), BF16)]
        out_specs += [pl.BlockSpec((1, 1, d), lambda i: (i, 0, 0)), row,
                      pl.BlockSpec((1, d, tm), lambda i: (i, 0, 0))]
    return pl.pallas_call(
        functools.partial(_kv_kernel, block_sums=block_sums),
        out_shape=out_shape,
        grid=(t // tm,),
        in_specs=[row, _resident((1, d)), _resident(w.shape), tab, tab, tab],
        out_specs=out_specs,
        compiler_params=_cparams("parallel"),
        name="kv_proj",
    )(h, g.reshape(1, d), w, *tables)


def _in_proj_kernel(x_ref, g_ref, w_ref, wdt_ref, cw_ref, cb_ref, z_ref, xbc_ref, dt_ref, tail_ref, ext_ref,
                    *, tiles_per_seq):
    tm = x_ref.shape[0]
    xn = _rms(x_ref[...], g_ref[...]).astype(BF16)

    @pl.when(pl.program_id(0) % tiles_per_seq == 0)
    def _():
        ext_ref[...] = jnp.zeros((SUBLANES, SSM_CONV_DIM), F32)

    first_rows = lax.broadcasted_iota(jnp.int32, (SUBLANES, PROJ_CHUNK), 0)

    def xbc_chunk(c0):
        return _dot(xn, w_ref[:, D_INNER + c0:D_INNER + c0 + PROJ_CHUNK])

    other = [(z_ref, c0, w_ref) for c0 in range(0, D_INNER, PROJ_CHUNK)]
    other.append((dt_ref, 0, wdt_ref))
    pending = xbc_chunk(0)
    for c0 in range(0, SSM_CONV_DIM, PROJ_CHUNK):
        cols = slice(c0, c0 + PROJ_CHUNK)
        raw = pending
        if c0 + PROJ_CHUNK < SSM_CONV_DIM:
            pending = xbc_chunk(c0 + PROJ_CHUNK)
        if other:
            o_ref, dst, src_ref = other.pop(0)
            o_ref[:, dst:dst + PROJ_CHUNK] = _dot(xn, src_ref[:, dst:dst + PROJ_CHUNK]).astype(o_ref.dtype)
        prev = ext_ref[:, cols]
        acc = cb_ref[:, cols] + cw_ref[SSM_CONV_W - 1:SSM_CONV_W, cols] * raw
        for back in range(1, SSM_CONV_W):
            cur = pltpu.roll(raw, back, axis=0)
            head = jnp.where(first_rows < back, pltpu.roll(prev, back, axis=0), cur[:SUBLANES])
            shifted = jnp.concatenate([head, cur[SUBLANES:]], axis=0)
            acc = acc + cw_ref[SSM_CONV_W - 1 - back:SSM_CONV_W - back, cols] * shifted
        xbc_ref[:, cols] = _silu(acc).astype(xbc_ref.dtype)
        tail = raw[tm - SUBLANES:]
        tail_ref[0, :, cols] = tail
        ext_ref[:, cols] = tail


def _in_proj(h, g, w, w_dt, conv_w, conv_b, tm, seq):
    t, d = h.shape
    tiles_per_seq = seq // tm
    row = lambda n: pl.BlockSpec((tm, n), lambda i: (i, 0))
    return pl.pallas_call(
        functools.partial(_in_proj_kernel, tiles_per_seq=tiles_per_seq),
        out_shape=[jax.ShapeDtypeStruct((t, D_INNER), BF16),
                   jax.ShapeDtypeStruct((t, SSM_CONV_DIM), BF16),
                   jax.ShapeDtypeStruct((t, SSM_GROUPS * LANES), F32),
                   jax.ShapeDtypeStruct((t // seq, SUBLANES, SSM_CONV_DIM), F32)],
        grid=(t // tm,),
        in_specs=[row(d), _resident((1, d)), _resident(w.shape), _resident(w_dt.shape),
                  _resident(conv_w.shape), _resident((1, SSM_CONV_DIM))],
        out_specs=[row(D_INNER), row(SSM_CONV_DIM), row(SSM_GROUPS * LANES),
                   pl.BlockSpec((1, SUBLANES, SSM_CONV_DIM), lambda i: (i // tiles_per_seq, 0, 0))],
        scratch_shapes=[pltpu.VMEM((SUBLANES, SSM_CONV_DIM), F32)],
        compiler_params=_cparams("arbitrary"),
        name="in_proj",
    )(h, g.reshape(1, d), w, w_dt, conv_w, conv_b.reshape(1, SSM_CONV_DIM))


def _split_dot(x, w, terms):
    return _dot(jnp.concatenate(_split3(x)[:terms], axis=1), w)


SSD_GROUPS_PER_STEP = 4


def _ssd_kernel(x_ref, b_ref, c_ref, dt_ref, dtb_ref, alog_ref, dsk_ref, spread_ref, select_ref,
                y_ref, st_ref):
    cs = SSD_CHUNK

    @pl.when(pl.program_id(2) == 0)
    def _():
        st_ref[...] = jnp.zeros(st_ref.shape, F32)

    row = lax.broadcasted_iota(jnp.int32, (cs, cs), 0)
    col = lax.broadcasted_iota(jnp.int32, (cs, cs), 1)
    causal = row >= col
    tril = jnp.where(causal, 1.0, 0.0).astype(BF16)
    low_half = col < SSM_HEAD_DIM
    spread = spread_ref[...]

    for g in range(SSD_GROUPS_PER_STEP):
        heads = slice(g * HEADS_PER_GROUP, (g + 1) * HEADS_PER_GROUP)
        xs = x_ref[0, :, g * GROUP_X:(g + 1) * GROUP_X].astype(F32)
        bm16 = b_ref[0, :, g * SSM_D_STATE:(g + 1) * SSM_D_STATE]
        cm16 = c_ref[0, :, g * SSM_D_STATE:(g + 1) * SSM_D_STATE]

        dt = jax.nn.softplus(dt_ref[:, g * LANES:(g + 1) * LANES] + dtb_ref[g])
        a = jnp.where(col < HEADS_PER_GROUP, dt * (-jnp.exp(alog_ref[g])), 0.0)
        a_hi, a_mid, a_lo = _split3(a)
        acum = _dot(tril, a_hi) + _dot(tril, a_mid) + _dot(tril, a_lo)
        acum_t = acum.T
        ac_last = acum[cs - 1:cs, :]

        xdt = xs * _split_dot(dt, spread, 2)
        xdt16 = xdt.astype(BF16)
        xw = xdt * _split_dot(jnp.exp(ac_last - acum), spread, 2)
        ac_col = _split_dot(acum, select_ref[...], 3)

        cb = _dot_nt(cm16, bm16)
        st_old = st_ref[0, heads].reshape(GROUP_X, SSM_D_STATE)
        y = _dot_nt(cm16, st_old.astype(BF16)) * _split_dot(jnp.exp(acum), spread, 2) + dsk_ref[g] * xs

        for pair in range(HEADS_PER_GROUP // 2):
            x_pair = xdt16[:, pair * LANES:(pair + 1) * LANES]
            y_pair = None
            for half in range(2):
                e = 2 * pair + half
                seg = ac_col[:, e * LANES:(e + 1) * LANES] - acum_t[e:e + 1, :]
                lmat = jnp.exp(jnp.where(causal, seg, NEG_INF))
                x_half = jnp.where(low_half if half == 0 else ~low_half, x_pair, jnp.zeros_like(x_pair))
                part = _dot((cb * lmat).astype(BF16), x_half)
                y_pair = part if y_pair is None else y_pair + part
            lanes = slice(g * GROUP_X + pair * LANES, g * GROUP_X + (pair + 1) * LANES)
            y_ref[0, :, lanes] = (y[:, pair * LANES:(pair + 1) * LANES] + y_pair).astype(y_ref.dtype)

        states = _dot(xw.T.astype(BF16), bm16)
        for e in range(HEADS_PER_GROUP):
            lo, hi = e * SSM_HEAD_DIM, (e + 1) * SSM_HEAD_DIM
            head = g * HEADS_PER_GROUP + e
            st_ref[0, head] = st_ref[0, head] * jnp.exp(ac_last[:, e:e + 1]) + states[lo:hi, :]


def _group_lanes(v):
    v = v.astype(F32).reshape(SSM_GROUPS, 1, HEADS_PER_GROUP)
    return jnp.pad(v, ((0, 0), (0, 0), (0, LANES - HEADS_PER_GROUP)))


def _ssd_prompt(xbc, dt, dt_bias, a_log, d_skip, n_seq, seq):
    nc = seq // SSD_CHUNK
    cs = SSD_CHUNK
    xbc = xbc.reshape(n_seq, seq, SSM_CONV_DIM)
    nb = D_INNER // SSM_D_STATE
    par = _group_lanes
    src = jnp.arange(LANES, dtype=jnp.int32)[:, None]
    spread = (jnp.arange(GROUP_X, dtype=jnp.int32)[None, :] // SSM_HEAD_DIM == src).astype(BF16)
    select = (jnp.arange(HEADS_PER_GROUP * LANES, dtype=jnp.int32)[None, :] // LANES == src).astype(BF16)
    spread2, select3 = jnp.tile(spread, (2, 1)), jnp.tile(select, (3, 1))
    d_chan = jnp.repeat(d_skip.astype(F32), SSM_HEAD_DIM).reshape(SSM_GROUPS, 1, GROUP_X)
    gps = SSD_GROUPS_PER_STEP
    b_blk = nb // gps
    c_blk = (nb + SSM_GROUPS) // gps

    def lanes(width, idx):
        return pl.BlockSpec((1, cs, gps * width), idx)

    def per_group(width):
        return pl.BlockSpec((gps, 1, width), lambda n, g, c: (g, 0, 0))

    in_specs = [
        lanes(GROUP_X, lambda n, g, c: (n, c, g)),
        lanes(SSM_D_STATE, lambda n, g, c: (n, c, b_blk + g)),
        lanes(SSM_D_STATE, lambda n, g, c: (n, c, c_blk + g)),
        pl.BlockSpec((cs, gps * LANES), lambda n, g, c: (n * nc + c, g)),
        per_group(LANES),
        per_group(LANES),
        per_group(GROUP_X),
        _resident(spread2.shape),
        _resident(select3.shape),
    ]
    return pl.pallas_call(
        _ssd_kernel,
        out_shape=[jax.ShapeDtypeStruct((n_seq, seq, D_INNER), BF16),
                   jax.ShapeDtypeStruct((n_seq, SSM_HEADS, SSM_HEAD_DIM, SSM_D_STATE), F32)],
        grid=(n_seq, SSM_GROUPS // gps, nc),
        in_specs=in_specs,
        out_specs=[lanes(GROUP_X, lambda n, g, c: (n, c, g)),
                   pl.BlockSpec((1, gps * HEADS_PER_GROUP, SSM_HEAD_DIM, SSM_D_STATE),
                                lambda n, g, c: (n, g, 0, 0))],
        compiler_params=_cparams("parallel", "parallel", "arbitrary"),
        name="ssd_prompt",
    )(xbc, xbc, xbc, dt, par(dt_bias), par(a_log), d_chan, spread2, select3)


def _ssm_step_kernel(xbc_ref, cst_ref, dt_ref, w_ref, b_ref, dtb_ref, alog_ref, dsk_ref, *refs):
    st_refs, (y_ref, sto_ref) = refs[:SSM_GROUPS], refs[SSM_GROUPS:]
    w = w_ref[...]
    acc = b_ref[...] + w[SSM_CONV_W - 1:SSM_CONV_W, :] * xbc_ref[0]
    for k in range(SSM_CONV_W - 1):
        acc = acc + w[k:k + 1, :] * cst_ref[0, k:k + 1, :]
    act = _silu(acc)
    nb = D_INNER
    row0 = lax.broadcasted_iota(jnp.int32, (2 * SUBLANES, 1), 0) == 0

    def dot3(a, b, dims):
        a_hi, a_mid, _ = _split3(a)
        b_hi, b_mid, _ = _split3(b)
        dg = lambda u, v: lax.dot_general(u, v, (dims, ((), ())), preferred_element_type=F32)
        return dg(a_hi, b_hi) + dg(a_hi, b_mid) + dg(a_mid, b_hi)

    for g in range(SSM_GROUPS):
        xs = act[:, g * GROUP_X:(g + 1) * GROUP_X]
        bm = act[:, nb + g * SSM_D_STATE:nb + (g + 1) * SSM_D_STATE]
        cm = act[:, nb + (SSM_GROUPS + g) * SSM_D_STATE:nb + (SSM_GROUPS + g + 1) * SSM_D_STATE]
        dt = jax.nn.softplus(dt_ref[0, :, g * LANES:(g + 1) * LANES] + dtb_ref[g])
        decay = jnp.exp(dt * (-jnp.exp(alog_ref[g])))
        xdt = jnp.concatenate(
            [xs[:, e * SSM_HEAD_DIM:(e + 1) * SSM_HEAD_DIM] * dt[:, e:e + 1] for e in range(HEADS_PER_GROUP)], axis=1)
        x8 = jnp.where(row0, xdt, 0.0)
        b8 = jnp.where(row0, bm, 0.0)
        c8 = jnp.where(row0, cm, 0.0)
        outer = dot3(x8, b8, ((0,), (0,)))
        new = []
        for e in range(HEADS_PER_GROUP):
            head = g * HEADS_PER_GROUP + e
            h_new = st_refs[g][0, e] * decay[:, e:e + 1] + outer[e * SSM_HEAD_DIM:(e + 1) * SSM_HEAD_DIM, :]
            sto_ref[0, head] = h_new
            new.append(h_new)
        h_g = jnp.concatenate(new, axis=0)
        y8 = dot3(c8, h_g, ((1,), (1,)))
        dsk = dsk_ref[g]
        dx = jnp.concatenate(
            [xs[:, e * SSM_HEAD_DIM:(e + 1) * SSM_HEAD_DIM] * dsk[:, e:e + 1] for e in range(HEADS_PER_GROUP)], axis=1)
        y_ref[0, :, g * GROUP_X:(g + 1) * GROUP_X] = y8[0:1, :] + dx


def _ssm_step(xbc, dt, conv_state, ssm_states, layer, conv_w, conv_b, dt_bias, a_log, d_skip):
    n = xbc.shape[0]
    par = _group_lanes
    st_spec = pl.BlockSpec((1, SSM_HEADS, SSM_HEAD_DIM, SSM_D_STATE), lambda s: (s, 0, 0, 0))
    st_in_specs = [pl.BlockSpec((None, 1, HEADS_PER_GROUP, SSM_HEAD_DIM, SSM_D_STATE),
                                lambda s, g=g: (layer, s, g, 0, 0)) for g in range(SSM_GROUPS)]
    y, st = pl.pallas_call(
        _ssm_step_kernel,
        out_shape=[jax.ShapeDtypeStruct((n, 1, D_INNER), F32),
                   jax.ShapeDtypeStruct(ssm_states.shape[1:], F32)],
        grid=(n,),
        in_specs=[pl.BlockSpec((1, 1, SSM_CONV_DIM), lambda s: (s, 0, 0)),
                  pl.BlockSpec((1, SSM_CONV_W - 1, SSM_CONV_DIM), lambda s: (s, 0, 0)),
                  pl.BlockSpec((1, 1, SSM_GROUPS * LANES), lambda s: (s, 0, 0)),
                  _resident((SSM_CONV_W, SSM_CONV_DIM)),
                  _resident((1, SSM_CONV_DIM)),
                  _resident((SSM_GROUPS, 1, LANES)),
                  _resident((SSM_GROUPS, 1, LANES)),
                  _resident((SSM_GROUPS, 1, LANES)),
                  *st_in_specs],
        out_specs=[pl.BlockSpec((1, 1, D_INNER), lambda s: (s, 0, 0)), st_spec],
        compiler_params=_cparams("parallel"),
        name="ssm_step",
    )(xbc.reshape(n, 1, SSM_CONV_DIM), conv_state, dt.reshape(n, 1, SSM_GROUPS * LANES), conv_w,
      conv_b.reshape(1, SSM_CONV_DIM), par(dt_bias), par(a_log), par(d_skip), *([ssm_states] * SSM_GROUPS))
    return y.reshape(n, D_INNER), st


MOBA_HEADS_PER_STEP = 4


def _moba_kernel(q_ref, k_ref, vt_ref, mean_ref, o_ref, bias_ref):
    i = pl.program_id(2)
    blk = MOBA_BLOCK
    n_blk = mean_ref.shape[1]
    n_heads = q_ref.shape[2] // HEAD_DIM
    sub = lax.broadcasted_iota(jnp.int32, (n_blk, blk), 0)
    sub_f = sub.astype(F32)
    key = lax.broadcasted_iota(jnp.int32, (blk, blk), 0)
    qry = lax.broadcasted_iota(jnp.int32, (blk, blk), 1)
    own = pl.multiple_of(i * blk, blk)
    heads = [slice(hd * HEAD_DIM, (hd + 1) * HEAD_DIM) for hd in range(n_heads)]

    qs_t = []
    for hd, lanes in enumerate(heads):
        q_t = q_ref[0, :, lanes].T
        gate = jnp.dot(mean_ref[0, :, lanes], q_t, precision=lax.Precision.HIGHEST,
                       preferred_element_type=F32)
        past = sub < i
        gate = jnp.where(past, gate, NEG_INF)
        bias = jnp.full((n_blk, blk), NEG_INF, F32)
        for _ in range(MOBA_TOPK):
            top = jnp.max(gate, axis=0, keepdims=True)
            first = jnp.min(jnp.where(gate == top, sub_f, float(n_blk)), axis=0, keepdims=True)
            pick = (sub_f == first) & past
            bias = jnp.where(pick, 0.0, bias)
            gate = jnp.where(pick, NEG_INF, gate)
        for j in range(n_blk):
            bias_ref[hd, j] = bias[j:j + 1, :]
        qs_t.append((q_t * (HEAD_DIM ** -0.5)).astype(BF16))

    def pair_scores(hd, j0):
        off = j0 * blk if isinstance(j0, int) else pl.multiple_of(j0 * blk, 2 * blk)
        s = _dot(k_ref[0, pl.ds(off, 2 * blk), heads[hd]], qs_t[hd])
        return s[:blk] + bias_ref[hd, j0], s[blk:] + bias_ref[hd, j0 + 1]

    s_own = [_dot(k_ref[0, pl.ds(own, blk), lanes], qs_t[hd]) for hd, lanes in enumerate(heads)]
    s_pair = [pair_scores(hd, 0) for hd in range(n_heads)]
    carry = []
    for hd, lanes in enumerate(heads):
        s = jnp.where(key <= qry, s_own[hd], NEG_INF)
        m = jnp.max(s, axis=0, keepdims=True)
        p = jnp.exp(s - m)
        l = jnp.sum(p, axis=0, keepdims=True)
        acc = _dot(vt_ref[0, i, lanes, :], p.astype(BF16))
        carry.append((m, l, acc) + s_pair[hd])

    def body(jj, carry):
        j0 = 2 * jj
        j_next = jnp.minimum(j0 + 2, n_blk - 2)
        s_next = [pair_scores(hd, j_next) for hd in range(n_heads)]
        out = []
        for hd, lanes in enumerate(heads):
            m, l, acc, s0, s1 = carry[hd]
            m_new = jnp.maximum(m, jnp.max(jnp.maximum(s0, s1), axis=0, keepdims=True))
            alpha = jnp.exp(m - m_new)
            p0 = jnp.exp(s0 - m_new)
            p1 = jnp.exp(s1 - m_new)
            l = alpha * l + jnp.sum(p0 + p1, axis=0, keepdims=True)
            v_pair = jnp.concatenate([vt_ref[0, j0, lanes, :], vt_ref[0, j0 + 1, lanes, :]], axis=1)
            p_pair = jnp.concatenate([p0.astype(BF16), p1.astype(BF16)], axis=0)
            acc = alpha * acc + _dot(v_pair, p_pair)
            out.append((m_new, l, acc) + s_next[hd])
        return tuple(out)

    carry = lax.fori_loop(0, (i + 1) // 2, body, tuple(carry))
    for hd, lanes in enumerate(heads):
        m, l, acc = carry[hd][:3]
        o_ref[0, :, lanes] = (acc / l).T.astype(o_ref.dtype)


def _moba_prompt(q, k16, v_t, kmean, n_seq, seq):
    n_blk = seq // MOBA_BLOCK
    width = MOBA_HEADS_PER_STEP * HEAD_DIM
    tile = pl.BlockSpec((1, MOBA_BLOCK, width), lambda n, h, i: (n, i, h))
    return pl.pallas_call(
        _moba_kernel,
        out_shape=jax.ShapeDtypeStruct((n_seq, seq, D_MODEL), BF16),
        grid=(n_seq, N_HEADS // MOBA_HEADS_PER_STEP, n_blk),
        in_specs=[tile,
                  pl.BlockSpec((1, seq, width), lambda n, h, i: (n, 0, h)),
                  pl.BlockSpec((1, n_blk, width, MOBA_BLOCK), lambda n, h, i: (n, 0, h, 0)),
                  pl.BlockSpec((1, n_blk, width), lambda n, h, i: (n, 0, h))],
        out_specs=tile,
        scratch_shapes=[pltpu.VMEM((MOBA_HEADS_PER_STEP, n_blk, 1, MOBA_BLOCK), F32)],
        compiler_params=_cparams("parallel", "parallel", "arbitrary"),
        name="moba_prompt",
    )(q, k16, v_t, kmean)


MEAN_BLOCKS_PER_STEP = 4


def _page_mean_kernel(pt_ref, *refs):
    del pt_ref
    o_ref = refs[-1]
    for b in range(MEAN_BLOCKS_PER_STEP):
        total = jnp.sum(refs[PAGES_PER_BLOCK * b][0], axis=0)
        for t in range(1, PAGES_PER_BLOCK):
            total = total + jnp.sum(refs[PAGES_PER_BLOCK * b + t][0], axis=0)
        o_ref[0, b] = total * (1.0 / MOBA_BLOCK)


def _page_means(cache_k, page_table):
    n_seq, n_pages = page_table.shape
    n_blk = n_pages // PAGES_PER_BLOCK
    per_step = MEAN_BLOCKS_PER_STEP * PAGES_PER_BLOCK
    page = lambda t: pl.BlockSpec(
        (1, PAGE_SIZE, N_HEADS, HEAD_DIM), lambda s, b, pt: (pt[s * n_pages + per_step * b + t], 0, 0, 0))
    out = pl.pallas_call(
        _page_mean_kernel,
        out_shape=jax.ShapeDtypeStruct((n_seq, n_blk, N_HEADS, HEAD_DIM), F32),
        grid_spec=pltpu.PrefetchScalarGridSpec(
            num_scalar_prefetch=1,
            grid=(n_seq, n_blk // MEAN_BLOCKS_PER_STEP),
            in_specs=[page(t) for t in range(per_step)],
            out_specs=pl.BlockSpec((1, MEAN_BLOCKS_PER_STEP, N_HEADS, HEAD_DIM), lambda s, b, pt: (s, b, 0, 0)),
        ),
        compiler_params=_cparams("parallel", "parallel"),
        name="page_means",
    )(page_table.reshape(-1), *([cache_k] * per_step))
    return out.reshape(n_seq, n_blk, D_MODEL)


def _select_kernel(q_ref, mean_ref, idx_ref):
    n_blk = mean_ref.shape[1]
    prod = mean_ref[0] * q_ref[0]
    lane = lax.broadcasted_iota(jnp.int32, (n_blk, LANES), 1)
    sub = lax.broadcasted_iota(jnp.int32, (n_blk, LANES), 0)
    gate = jnp.zeros((n_blk, LANES), F32)
    for hd in range(N_HEADS):
        g_h = jnp.sum(prod[:, hd * HEAD_DIM:(hd + 1) * HEAD_DIM], axis=1, keepdims=True)
        gate = jnp.where(lane == hd, g_h, gate)
    out_sub = lax.broadcasted_iota(jnp.int32, (SUBLANES, LANES), 0)
    out = jnp.zeros((SUBLANES, LANES), jnp.int32)
    for j in range(n_blk):
        g_j = gate[j:j + 1, :]
        beats = (gate > g_j) | ((gate == g_j) & (sub < j))
        rank = jnp.sum(jnp.where(beats, 1.0, 0.0), axis=0, keepdims=True)
        for r in range(MOBA_TOPK):
            out = jnp.where((out_sub == r) & (rank == float(r)), j, out)
    idx_ref[0] = out


def _select_blocks(q, means):
    n_seq, n_blk, _ = means.shape
    idx = pl.pallas_call(
        _select_kernel,
        out_shape=jax.ShapeDtypeStruct((n_seq, SUBLANES, LANES), jnp.int32),
        grid=(n_seq,),
        in_specs=[pl.BlockSpec((1, 1, D_MODEL), lambda s: (s, 0, 0)),
                  pl.BlockSpec((1, n_blk, D_MODEL), lambda s: (s, 0, 0))],
        out_specs=pl.BlockSpec((1, SUBLANES, LANES), lambda s: (s, 0, 0)),
        compiler_params=_cparams("parallel"),
        name="select_blocks",
    )(q.reshape(n_seq, 1, D_MODEL), means)
    return idx[:, :MOBA_TOPK, :N_HEADS]


N_SEL_PAGES = MOBA_TOPK * PAGES_PER_BLOCK


def _decode_attn_kernel(pt_ref, idx_ref, q_ref, kn_ref, vn_ref, k_hbm, v_hbm, o_ref, kbuf, vbuf, sems,
                        *, n_pages):
    s_i = pl.program_id(0)
    n_seq = pl.num_programs(0)
    buf = s_i % 2

    def copies(seq, hd, slot):
        r, t = divmod(slot, PAGES_PER_BLOCK)
        blk = idx_ref[(seq * MOBA_TOPK + r) * N_HEADS + hd]
        pg = pt_ref[seq * n_pages + PAGES_PER_BLOCK * blk + t]
        half = seq % 2
        return (pltpu.make_async_copy(k_hbm.at[pg, :, hd, :], kbuf.at[half, hd, slot], sems.at[half, 0, hd, slot]),
                pltpu.make_async_copy(v_hbm.at[pg, :, hd, :], vbuf.at[half, hd, slot], sems.at[half, 1, hd, slot]))

    def start_all(seq):
        for hd in range(N_HEADS):
            for slot in range(N_SEL_PAGES):
                for cp in copies(seq, hd, slot):
                    cp.start()

    @pl.when(s_i == 0)
    def _():
        start_all(s_i)

    @pl.when(s_i + 1 < n_seq)
    def _():
        start_all(s_i + 1)

    for hd in range(N_HEADS):
        lanes = slice(hd * HEAD_DIM, (hd + 1) * HEAD_DIM)
        for slot in range(N_SEL_PAGES):
            for cp in copies(s_i, hd, slot):
                cp.wait()
        q = q_ref[0, :, lanes] * (HEAD_DIM ** -0.5)
        s_own = jnp.sum(q * kn_ref[0, :, lanes], axis=1, keepdims=True)
        scores = [jnp.sum(kbuf[buf, hd, slot] * q, axis=1, keepdims=True) for slot in range(N_SEL_PAGES)]
        m = s_own
        for s in scores:
            m = jnp.maximum(m, jnp.max(s, axis=0, keepdims=True))
        p_own = jnp.exp(s_own - m)
        l = p_own
        acc = p_own * vn_ref[0, :, lanes]
        for slot, s in enumerate(scores):
            p = jnp.exp(s - m)
            l = l + jnp.sum(p, axis=0, keepdims=True)
            acc = acc + jnp.sum(p * vbuf[buf, hd, slot], axis=0, keepdims=True)
        o_ref[0, :, lanes] = acc / l


def _decode_attn(q, k_new, v_new, cache_k, cache_v, page_table, idx):
    n_seq, n_pages = page_table.shape
    vec = pl.BlockSpec((1, 1, D_MODEL), lambda s, pt, sel: (s, 0, 0))
    hbm = pl.BlockSpec(memory_space=pl.ANY)
    r3 = lambda a: a.reshape(n_seq, 1, D_MODEL)
    slabs = (2, N_HEADS, N_SEL_PAGES, PAGE_SIZE, HEAD_DIM)
    out = pl.pallas_call(
        functools.partial(_decode_attn_kernel, n_pages=n_pages),
        out_shape=jax.ShapeDtypeStruct((n_seq, 1, D_MODEL), F32),
        grid_spec=pltpu.PrefetchScalarGridSpec(
            num_scalar_prefetch=2,
            grid=(n_seq,),
            in_specs=[vec, vec, vec, hbm, hbm],
            out_specs=vec,
            scratch_shapes=[pltpu.VMEM(slabs, F32), pltpu.VMEM(slabs, F32),
                            pltpu.SemaphoreType.DMA((2, 2, N_HEADS, N_SEL_PAGES))],
        ),
        compiler_params=_cparams("arbitrary"),
        name="decode_attn",
    )(page_table.reshape(-1), idx.reshape(-1), r3(q), r3(k_new), r3(v_new), cache_k, cache_v)
    return out.reshape(n_seq, D_MODEL)


def _cast_kernel(*refs, n_parts):
    w_refs, o_refs = refs[:-n_parts], refs[-n_parts:]
    streams = len(w_refs) // n_parts
    rows = w_refs[0].shape[0]
    for p, o_ref in enumerate(o_refs):
        for r in range(streams):
            o_ref[r * rows:(r + 1) * rows, :] = w_refs[p * streams + r][...].astype(o_ref.dtype)


CAST_STREAMS = 4


def _to_bf16(w, layer, row_tile, col_parts=1):
    _, rows, cols = w.shape
    streams = CAST_STREAMS if rows % (CAST_STREAMS * row_tile) == 0 else 1
    width = cols // col_parts
    in_specs = [pl.BlockSpec((None, row_tile, width), lambda i, p=p, r=r: (layer, i * streams + r, p))
                for p in range(col_parts) for r in range(streams)]
    out = pl.pallas_call(
        functools.partial(_cast_kernel, n_parts=col_parts),
        out_shape=[jax.ShapeDtypeStruct((rows, width), BF16)] * col_parts,
        grid=(rows // (streams * row_tile),),
        in_specs=in_specs,
        out_specs=[pl.BlockSpec((streams * row_tile, width), lambda i: (i, 0))] * col_parts,
        compiler_params=_cparams("parallel"),
        name="to_bf16",
    )(*([w] * len(in_specs)))
    return out[0] if col_parts == 1 else tuple(out)


def _dt_weight(w_in):
    w_dt = w_in[:, D_INNER + SSM_CONV_DIM:].reshape(D_MODEL, SSM_GROUPS, HEADS_PER_GROUP)
    w_dt = jnp.pad(w_dt, ((0, 0), (0, 0), (0, LANES - HEADS_PER_GROUP)))
    return _to_bf16(w_dt.reshape(1, D_MODEL, SSM_GROUPS * LANES), 0, D_MODEL)


W_ROW_TILE = 32
W_DOWN_ROW_TILE = D_FF // 16


def kernel(x_prompt, x_sample, state_conv, state_ssm, cache_k, cache_v, page_table, norm_ffn_a, w_ffn_a_up, w_ffn_a_down, norm_mix, norm_ffn_b, w_ffn_b_up, w_ffn_b_down, ssm_w_in, ssm_conv_w, ssm_conv_b, ssm_dt_bias, ssm_a_log, ssm_d, ssm_norm, ssm_w_out, norm_kv, w_kv, w_q, w_o, norm_final):
    n_p, seq, d = x_prompt.shape
    n_s = x_sample.shape[0]
    past_len = page_table.shape[1] * PAGE_SIZE
    tm_p = 512
    tm_s = n_s

    hp = x_prompt.reshape(n_p * seq, d)
    hs = x_sample.reshape(n_s, d)
    tab_p = _rope_tables(jnp.arange(seq, dtype=jnp.int32))
    tab_s = _rope_tables(jnp.full((n_s,), past_len, jnp.int32))

    conv_p, ssm_p, conv_s, ssm_s = [], [], [], []
    k_p = v_p = k_s = v_s = None
    for layer in range(DEPTH):
        if layer == N_A_LAYERS:
            w_kv16 = _to_bf16(w_kv[None], 0, W_ROW_TILE)
            k_p, v_p, kmean_p, k16, v_t = _kv_proj(hp, norm_kv, w_kv16, tab_p, MOBA_BLOCK, True)
            k_s, v_s = _kv_proj(hs, norm_kv, w_kv16, tab_s, tm_s, False)
            k16 = k16.reshape(n_p, seq, d)
            v_t = v_t.reshape(n_p, seq // MOBA_BLOCK, d, MOBA_BLOCK)
            kmean_p = kmean_p.reshape(n_p, seq // MOBA_BLOCK, d)
            kmean_s = _page_means(cache_k, page_table)

        w_up = _to_bf16(w_ffn_a_up, layer, W_ROW_TILE, 2)
        w_dn = _to_bf16(w_ffn_a_down, layer, W_DOWN_ROW_TILE)
        hp = _ffn(hp, norm_ffn_a[layer], w_up, w_dn, tm_p)
        hs = _ffn(hs, norm_ffn_a[layer], w_up, w_dn, tm_s)

        if layer < N_A_LAYERS:
            w_in = _to_bf16(ssm_w_in, layer, W_ROW_TILE)
            w_dt = _dt_weight(ssm_w_in[layer])
            w_out = _to_bf16(ssm_w_out, layer, W_ROW_TILE)
            ssm_args = (ssm_conv_w[layer], ssm_conv_b[layer], ssm_dt_bias[layer], ssm_a_log[layer], ssm_d[layer])
            z, xbc, dt, tail = _in_proj(hp, norm_mix[layer], w_in, w_dt, ssm_conv_w[layer], ssm_conv_b[layer],
                                        tm_p, seq)
            y, st = _ssd_prompt(xbc, dt, *ssm_args[2:], n_p, seq)
            conv_p.append(tail[:, SUBLANES - (SSM_CONV_W - 1):])
            ssm_p.append(st)
            mix_p = (y.reshape(n_p * seq, D_INNER), w_out, z, ssm_norm[layer])

            z, xbc = _proj(hs, norm_mix[layer], w_in, (D_INNER, SSM_CONV_DIM), tm_s)
            dt, = _proj(hs, norm_mix[layer], w_dt, (SSM_GROUPS * LANES,), tm_s)
            y, st = _ssm_step(xbc, dt, state_conv[layer], state_ssm, layer, *ssm_args)
            conv_s.append(jnp.concatenate([state_conv[layer][:, 1:], xbc[:, None, :]], axis=1))
            ssm_s.append(st)
            mix_s = (y, w_out, z, ssm_norm[layer])
        else:
            j = layer - N_A_LAYERS
            w_q16, w_o16 = _to_bf16(w_q, j, W_ROW_TILE), _to_bf16(w_o, j, W_ROW_TILE)
            q = _q_proj(hp, norm_mix[layer], w_q16, tab_p, tm_p)
            o = _moba_prompt(q.reshape(n_p, seq, d), k16, v_t, kmean_p, n_p, seq)
            mix_p = (o.reshape(n_p * seq, d), w_o16)

            q = _q_proj(hs, norm_mix[layer], w_q16, tab_s, tm_s)
            idx = _select_blocks(q, kmean_s)
            o = _decode_attn(q, k_s, v_s, cache_k, cache_v, page_table, idx)
            mix_s = (o, w_o16)

        w_up = _to_bf16(w_ffn_b_up, layer, W_ROW_TILE, 2)
        w_dn = _to_bf16(w_ffn_b_down, layer, W_DOWN_ROW_TILE)
        final_g = norm_final if layer == DEPTH - 1 else None
        hp = _ffn(hp, norm_ffn_b[layer], w_up, w_dn, tm_p, final_g, mix_p)
        hs = _ffn(hs, norm_ffn_b[layer], w_up, w_dn, tm_s, final_g, mix_s)

    heads = (N_HEADS, HEAD_DIM)
    return (hp.reshape(n_p, seq, d), hs.reshape(n_s, 1, d),
            jnp.stack(conv_p), jnp.stack(ssm_p),
            k_p.reshape(n_p, seq, *heads), v_p.reshape(n_p, seq, *heads),
            jnp.stack(conv_s), jnp.stack(ssm_s),
            k_s.reshape(n_s, 1, *heads), v_s.reshape(n_s, 1, *heads))
```

```python
import functools
import math

import jax
import jax.numpy as jnp
from jax import lax
from jax.experimental import pallas as pl
from jax.experimental.pallas import tpu as pltpu

F32 = jnp.float32
BF16 = jnp.bfloat16

D_MODEL = 1024
DEPTH = 4
N_A_LAYERS = 2
NORM_EPS = 1e-5
D_FF = 2816
D_INNER = 2048
SSM_HEAD_DIM = 64
SSM_HEADS = 32
SSM_GROUPS = 4
HEADS_PER_GROUP = 8
SSM_D_STATE = 128
SSM_CONV_W = 4
SSM_CONV_DIM = D_INNER + 2 * SSM_GROUPS * SSM_D_STATE
SSD_CHUNK = 128
GROUP_X = HEADS_PER_GROUP * SSM_HEAD_DIM
N_HEADS = 8
HEAD_DIM = 128
ROT_DIM = 32
ROPE_THETA = 500000.0
MOBA_BLOCK = 256
MOBA_TOPK = 3
PAGE_SIZE = 128
PAGES_PER_BLOCK = MOBA_BLOCK // PAGE_SIZE

LANES = 128
SUBLANES = 8
VMEM_LIMIT_BYTES = 56 * 1024 * 1024
FF_CHUNK = 256
NEG_INF = float("-inf")


def _cparams(*sem):
    return pltpu.CompilerParams(dimension_semantics=sem, vmem_limit_bytes=VMEM_LIMIT_BYTES)


def _resident(shape):
    nd = len(shape)
    return pl.BlockSpec(shape, lambda *_: (0,) * nd, pipeline_mode=pl.Buffered(1))


def _rms(x, g):
    return x * lax.rsqrt(jnp.mean(x * x, axis=-1, keepdims=True) + NORM_EPS) * g


def _silu(x):
    return x * jax.nn.sigmoid(x)


def _split3(x):
    hi = x.astype(BF16)
    r1 = x - hi.astype(F32)
    mid = r1.astype(BF16)
    lo = (r1 - mid.astype(F32)).astype(BF16)
    return hi, mid, lo


def _dot(a, b):
    return jnp.dot(a, b, preferred_element_type=F32)


def _dot_nt(a, b):
    return lax.dot_general(a, b, (((1,), (1,)), ((), ())), preferred_element_type=F32)


def _ffn_kernel(*refs, mixer, final_norm):
    refs = list(refs)
    o_ref = refs.pop()
    gf_ref = refs.pop() if final_norm else None
    x_ref, g_ref, wg_ref, wu_ref, wda_ref, wdb_ref = refs[:6]
    x = x_ref[...]
    if mixer == "gated":
        y_ref, wo_ref, z_ref, nw_ref = refs[6:]
        y = _rms(y_ref[...].astype(F32) * _silu(z_ref[...].astype(F32)), nw_ref[...])
        x = x + _dot(y.astype(BF16), wo_ref[...])
    elif mixer == "plain":
        y_ref, wo_ref = refs[6:]
        x = x + _dot(y_ref[...].astype(BF16), wo_ref[...])
    xn = _rms(x, g_ref[...]).astype(BF16)
    acc = jnp.zeros_like(x)
    for c in range(D_FF // FF_CHUNK):
        lo, hi = c * FF_CHUNK, (c + 1) * FF_CHUNK
        gate = _dot(xn, wg_ref[:, lo:hi])
        up = _dot(xn, wu_ref[:, lo:hi])
        act = (_silu(gate) * up).astype(BF16)
        acc = acc + jnp.concatenate([_dot(act, wda_ref[lo:hi, :]), _dot(act, wdb_ref[lo:hi, :])], axis=1)
    h = x + 0.5 * acc
    if final_norm:
        h = _rms(h, gf_ref[...])
    o_ref[...] = h


def _ffn(h, g, w_up, w_dn, tm, final_g=None, mix=None):
    t, d = h.shape
    row = pl.BlockSpec((tm, d), lambda i: (i, 0))
    w_gate, w_upper = w_up
    w_dn_a, w_dn_b = w_dn
    weights = [w_gate, w_upper, w_dn_a, w_dn_b]
    in_specs = [row, _resident((1, d))] + [_resident(w.shape) for w in weights]
    args = [h, g.reshape(1, d)] + weights
    mixer = None
    if mix is not None:
        y, w_o = mix[:2]
        k = y.shape[1]
        row_k = pl.BlockSpec((tm, k), lambda i: (i, 0))
        in_specs += [row_k, _resident(w_o.shape)]
        args += [y, w_o]
        mixer = "plain"
        if len(mix) == 4:
            in_specs += [row_k, _resident((1, k))]
            args += [mix[2], mix[3].reshape(1, k)]
            mixer = "gated"
    if final_g is not None:
        in_specs.append(_resident((1, d)))
        args.append(final_g.reshape(1, d))
    return pl.pallas_call(
        functools.partial(_ffn_kernel, mixer=mixer, final_norm=final_g is not None),
        out_shape=jax.ShapeDtypeStruct((t, d), F32),
        grid=(t // tm,),
        in_specs=in_specs,
        out_specs=row,
        compiler_params=_cparams("parallel"),
        name="ffn",
    )(*args)


PROJ_CHUNK = 512


def _store_proj(xn, w_ref, o_ref, col0, width):
    for c0 in range(0, width, PROJ_CHUNK):
        c1 = min(c0 + PROJ_CHUNK, width)
        o_ref[:, c0:c1] = _dot(xn, w_ref[:, col0 + c0:col0 + c1]).astype(o_ref.dtype)


def _proj_kernel(x_ref, g_ref, w_ref, *o_refs):
    xn = _rms(x_ref[...], g_ref[...]).astype(BF16)
    col = 0
    for o_ref in o_refs:
        width = o_ref.shape[1]
        _store_proj(xn, w_ref, o_ref, col, width)
        col += width


def _proj(h, g, w, widths, tm):
    t, d = h.shape
    return pl.pallas_call(
        _proj_kernel,
        out_shape=[jax.ShapeDtypeStruct((t, n), F32) for n in widths],
        grid=(t // tm,),
        in_specs=[pl.BlockSpec((tm, d), lambda i: (i, 0)), _resident((1, d)), _resident(w.shape)],
        out_specs=[pl.BlockSpec((tm, n), lambda i: (i, 0)) for n in widths],
        compiler_params=_cparams("parallel"),
        name="proj",
    )(h, g.reshape(1, d), w)


def _rope(y, cos, sin_lo, sin_hi):
    outs = []
    for hd in range(y.shape[1] // HEAD_DIM):
        xh = y[:, hd * HEAD_DIM:(hd + 1) * HEAD_DIM]
        up = pltpu.roll(xh, HEAD_DIM - ROT_DIM // 2, axis=1)
        dn = pltpu.roll(xh, ROT_DIM // 2, axis=1)
        outs.append(xh * cos + up * sin_lo + dn * sin_hi)
    return jnp.concatenate(outs, axis=1)


def _rope_tables(pos):
    half = ROT_DIM // 2
    inv = ROPE_THETA ** (-2.0 * jnp.arange(half, dtype=F32) / ROT_DIM)
    ang = pos.astype(F32)[:, None] * inv[None, :]
    cos, sin = jnp.cos(ang), jnp.sin(ang)
    n = pos.shape[0]
    ones = jnp.ones((n, HEAD_DIM - ROT_DIM), F32)
    zeros = jnp.zeros((n, HEAD_DIM - half), F32)
    cos_t = jnp.concatenate([cos, cos, ones], axis=1)
    sin_lo = jnp.concatenate([-sin, zeros], axis=1)
    sin_hi = jnp.concatenate([jnp.zeros((n, half), F32), sin, jnp.zeros((n, HEAD_DIM - ROT_DIM), F32)], axis=1)
    return cos_t, sin_lo, sin_hi


def _q_kernel(x_ref, g_ref, w_ref, cos_ref, slo_ref, shi_ref, q_ref):
    xn = _rms(x_ref[...], g_ref[...]).astype(BF16)
    q_ref[...] = _rope(_dot(xn, w_ref[...]), cos_ref[...], slo_ref[...], shi_ref[...])


def _q_proj(h, g, w, tables, tm):
    t, d = h.shape
    n_tab = tables[0].shape[0] // tm
    tab = pl.BlockSpec((tm, HEAD_DIM), lambda i: (i % n_tab, 0))
    row = pl.BlockSpec((tm, d), lambda i: (i, 0))
    return pl.pallas_call(
        _q_kernel,
        out_shape=jax.ShapeDtypeStruct((t, d), F32),
        grid=(t // tm,),
        in_specs=[row, _resident((1, d)), _resident(w.shape), tab, tab, tab],
        out_specs=row,
        compiler_params=_cparams("parallel"),
        name="q_proj",
    )(h, g.reshape(1, d), w, *tables)


def _kv_kernel(x_ref, g_ref, w_ref, cos_ref, slo_ref, shi_ref, k_ref, v_ref, *block_refs, block_sums):
    xn = _rms(x_ref[...], g_ref[...]).astype(BF16)
    k = _rope(_dot(xn, w_ref[:, :D_MODEL]), cos_ref[...], slo_ref[...], shi_ref[...])
    v = _dot(xn, w_ref[:, D_MODEL:])
    k_ref[...] = k
    v_ref[...] = v
    if block_sums:
        ksum_ref, k16_ref, vt16_ref = block_refs
        ksum_ref[0] = jnp.sum(k, axis=0, keepdims=True) * (1.0 / MOBA_BLOCK)
        k16_ref[...] = k.astype(BF16)
        vt16_ref[0] = v.T.astype(BF16)


def _kv_proj(h, g, w, tables, tm, block_sums):
    t, d = h.shape
    n_tab = tables[0].shape[0] // tm
    tab = pl.BlockSpec((tm, HEAD_DIM), lambda i: (i % n_tab, 0))
    row = pl.BlockSpec((tm, d), lambda i: (i, 0))
    out_shape = [jax.ShapeDtypeStruct((t, d), F32), jax.ShapeDtypeStruct((t, d), F32)]
    out_specs = [row, row]
    if block_sums:
        assert tm == MOBA_BLOCK
        out_shape += [jax.ShapeDtypeStruct((t // tm, 1, d), F32), jax.ShapeDtypeStruct((t, d), BF16),
                      jax.ShapeDtypeStruct((t // tm, d, tm), BF16)]
        out_specs += [pl.BlockSpec((1, 1, d), lambda i: (i, 0, 0)), row,
                      pl.BlockSpec((1, d, tm), lambda i: (i, 0, 0))]
    return pl.pallas_call(
        functools.partial(_kv_kernel, block_sums=block_sums),
        out_shape=out_shape,
        grid=(t // tm,),
        in_specs=[row, _resident((1, d)), _resident(w.shape), tab, tab, tab],
        out_specs=out_specs,
        compiler_params=_cparams("parallel"),
        name="kv_proj",
    )(h, g.reshape(1, d), w, *tables)


def _in_proj_kernel(x_ref, g_ref, w_ref, wdt_ref, cw_ref, cb_ref, z_ref, xbc_ref, dt_ref, tail_ref, ext_ref,
                    *, tiles_per_seq):
    tm = x_ref.shape[0]
    xn = _rms(x_ref[...], g_ref[...]).astype(BF16)

    @pl.when(pl.program_id(0) % tiles_per_seq == 0)
    def _():
        ext_ref[...] = jnp.zeros((SUBLANES, SSM_CONV_DIM), F32)

    first_rows = lax.broadcasted_iota(jnp.int32, (SUBLANES, PROJ_CHUNK), 0)

    def xbc_chunk(c0):
        return _dot(xn, w_ref[:, D_INNER + c0:D_INNER + c0 + PROJ_CHUNK])

    other = [(z_ref, c0, w_ref) for c0 in range(0, D_INNER, PROJ_CHUNK)]
    other.append((dt_ref, 0, wdt_ref))
    pending = xbc_chunk(0)
    for c0 in range(0, SSM_CONV_DIM, PROJ_CHUNK):
        cols = slice(c0, c0 + PROJ_CHUNK)
        raw = pending
        if c0 + PROJ_CHUNK < SSM_CONV_DIM:
            pending = xbc_chunk(c0 + PROJ_CHUNK)
        if other:
            o_ref, dst, src_ref = other.pop(0)
            o_ref[:, dst:dst + PROJ_CHUNK] = _dot(xn, src_ref[:, dst:dst + PROJ_CHUNK]).astype(o_ref.dtype)
        prev = ext_ref[:, cols]
        acc = cb_ref[:, cols] + cw_ref[SSM_CONV_W - 1:SSM_CONV_W, cols] * raw
        for back in range(1, SSM_CONV_W):
            cur = pltpu.roll(raw, back, axis=0)
            head = jnp.where(first_rows < back, pltpu.roll(prev, back, axis=0), cur[:SUBLANES])
            shifted = jnp.concatenate([head, cur[SUBLANES:]], axis=0)
            acc = acc + cw_ref[SSM_CONV_W - 1 - back:SSM_CONV_W - back, cols] * shifted
        xbc_ref[:, cols] = _silu(acc).astype(xbc_ref.dtype)
        tail = raw[tm - SUBLANES:]
        tail_ref[0, :, cols] = tail
        ext_ref[:, cols] = tail


def _in_proj(h, g, w, w_dt, conv_w, conv_b, tm, seq):
    t, d = h.shape
    tiles_per_seq = seq // tm
    row = lambda n: pl.BlockSpec((tm, n), lambda i: (i, 0))
    return pl.pallas_call(
        functools.partial(_in_proj_kernel, tiles_per_seq=tiles_per_seq),
        out_shape=[jax.ShapeDtypeStruct((t, D_INNER), BF16),
                   jax.ShapeDtypeStruct((t, SSM_CONV_DIM), BF16),
                   jax.ShapeDtypeStruct((t, SSM_GROUPS * LANES), F32),
                   jax.ShapeDtypeStruct((t // seq, SUBLANES, SSM_CONV_DIM), F32)],
        grid=(t // tm,),
        in_specs=[row(d), _resident((1, d)), _resident(w.shape), _resident(w_dt.shape),
                  _resident(conv_w.shape), _resident((1, SSM_CONV_DIM))],
        out_specs=[row(D_INNER), row(SSM_CONV_DIM), row(SSM_GROUPS * LANES),
                   pl.BlockSpec((1, SUBLANES, SSM_CONV_DIM), lambda i: (i // tiles_per_seq, 0, 0))],
        scratch_shapes=[pltpu.VMEM((SUBLANES, SSM_CONV_DIM), F32)],
        compiler_params=_cparams("arbitrary"),
        name="in_proj",
    )(h, g.reshape(1, d), w, w_dt, conv_w, conv_b.reshape(1, SSM_CONV_DIM))


def _split_dot(x, w, terms):
    return _dot(jnp.concatenate(_split3(x)[:terms], axis=1), w)


SSD_GROUPS_PER_STEP = 4


def _ssd_kernel(x_ref, b_ref, c_ref, dt_ref, dtb_ref, alog_ref, dsk_ref, spread_ref, select_ref,
                y_ref, st_ref):
    cs = SSD_CHUNK

    @pl.when(pl.program_id(2) == 0)
    def _():
        st_ref[...] = jnp.zeros(st_ref.shape, F32)

    row = lax.broadcasted_iota(jnp.int32, (cs, cs), 0)
    col = lax.broadcasted_iota(jnp.int32, (cs, cs), 1)
    causal = row >= col
    tril = jnp.where(causal, 1.0, 0.0).astype(BF16)
    low_half = col < SSM_HEAD_DIM
    spread = spread_ref[...]

    for g in range(SSD_GROUPS_PER_STEP):
        heads = slice(g * HEADS_PER_GROUP, (g + 1) * HEADS_PER_GROUP)
        xs = x_ref[0, :, g * GROUP_X:(g + 1) * GROUP_X].astype(F32)
        bm16 = b_ref[0, :, g * SSM_D_STATE:(g + 1) * SSM_D_STATE]
        cm16 = c_ref[0, :, g * SSM_D_STATE:(g + 1) * SSM_D_STATE]

        dt = jax.nn.softplus(dt_ref[:, g * LANES:(g + 1) * LANES] + dtb_ref[g])
        a = jnp.where(col < HEADS_PER_GROUP, dt * (-jnp.exp(alog_ref[g])), 0.0)
        a_hi, a_mid, a_lo = _split3(a)
        acum = _dot(tril, a_hi) + _dot(tril, a_mid) + _dot(tril, a_lo)
        acum_t = acum.T
        ac_last = acum[cs - 1:cs, :]

        xdt = xs * _split_dot(dt, spread, 2)
        xdt16 = xdt.astype(BF16)
        xw = xdt * _split_dot(jnp.exp(ac_last - acum), spread, 2)
        ac_col = _split_dot(acum, select_ref[...], 3)

        cb = _dot_nt(cm16, bm16)
        st_old = st_ref[0, heads].reshape(GROUP_X, SSM_D_STATE)
        y = _dot_nt(cm16, st_old.astype(BF16)) * _split_dot(jnp.exp(acum), spread, 2) + dsk_ref[g] * xs

        for pair in range(HEADS_PER_GROUP // 2):
            x_pair = xdt16[:, pair * LANES:(pair + 1) * LANES]
            y_pair = None
            for half in range(2):
                e = 2 * pair + half
                seg = ac_col[:, e * LANES:(e + 1) * LANES] - acum_t[e:e + 1, :]
                lmat = jnp.exp(jnp.where(causal, seg, NEG_INF))
                x_half = jnp.where(low_half if half == 0 else ~low_half, x_pair, jnp.zeros_like(x_pair))
                part = _dot((cb * lmat).astype(BF16), x_half)
                y_pair = part if y_pair is None else y_pair + part
            lanes = slice(g * GROUP_X + pair * LANES, g * GROUP_X + (pair + 1) * LANES)
            y_ref[0, :, lanes] = (y[:, pair * LANES:(pair + 1) * LANES] + y_pair).astype(y_ref.dtype)

        states = _dot(xw.T.astype(BF16), bm16)
        for e in range(HEADS_PER_GROUP):
            lo, hi = e * SSM_HEAD_DIM, (e + 1) * SSM_HEAD_DIM
            head = g * HEADS_PER_GROUP + e
            st_ref[0, head] = st_ref[0, head] * jnp.exp(ac_last[:, e:e + 1]) + states[lo:hi, :]


def _group_lanes(v):
    v = v.astype(F32).reshape(SSM_GROUPS, 1, HEADS_PER_GROUP)
    return jnp.pad(v, ((0, 0), (0, 0), (0, LANES - HEADS_PER_GROUP)))


def _ssd_prompt(xbc, dt, dt_bias, a_log, d_skip, n_seq, seq):
    nc = seq // SSD_CHUNK
    cs = SSD_CHUNK
    xbc = xbc.reshape(n_seq, seq, SSM_CONV_DIM)
    nb = D_INNER // SSM_D_STATE
    par = _group_lanes
    src = jnp.arange(LANES, dtype=jnp.int32)[:, None]
    spread = (jnp.arange(GROUP_X, dtype=jnp.int32)[None, :] // SSM_HEAD_DIM == src).astype(BF16)
    select = (jnp.arange(HEADS_PER_GROUP * LANES, dtype=jnp.int32)[None, :] // LANES == src).astype(BF16)
    spread2, select3 = jnp.tile(spread, (2, 1)), jnp.tile(select, (3, 1))
    d_chan = jnp.repeat(d_skip.astype(F32), SSM_HEAD_DIM).reshape(SSM_GROUPS, 1, GROUP_X)
    gps = SSD_GROUPS_PER_STEP
    b_blk = nb // gps
    c_blk = (nb + SSM_GROUPS) // gps

    def lanes(width, idx):
        return pl.BlockSpec((1, cs, gps * width), idx)

    def per_group(width):
        return pl.BlockSpec((gps, 1, width), lambda n, g, c: (g, 0, 0))

    in_specs = [
        lanes(GROUP_X, lambda n, g, c: (n, c, g)),
        lanes(SSM_D_STATE, lambda n, g, c: (n, c, b_blk + g)),
        lanes(SSM_D_STATE, lambda n, g, c: (n, c, c_blk + g)),
        pl.BlockSpec((cs, gps * LANES), lambda n, g, c: (n * nc + c, g)),
        per_group(LANES),
        per_group(LANES),
        per_group(GROUP_X),
        _resident(spread2.shape),
        _resident(select3.shape),
    ]
    return pl.pallas_call(
        _ssd_kernel,
        out_shape=[jax.ShapeDtypeStruct((n_seq, seq, D_INNER), BF16),
                   jax.ShapeDtypeStruct((n_seq, SSM_HEADS, SSM_HEAD_DIM, SSM_D_STATE), F32)],
        grid=(n_seq, SSM_GROUPS // gps, nc),
        in_specs=in_specs,
        out_specs=[lanes(GROUP_X, lambda n, g, c: (n, c, g)),
                   pl.BlockSpec((1, gps * HEADS_PER_GROUP, SSM_HEAD_DIM, SSM_D_STATE),
                                lambda n, g, c: (n, g, 0, 0))],
        compiler_params=_cparams("parallel", "parallel", "arbitrary"),
        name="ssd_prompt",
    )(xbc, xbc, xbc, dt, par(dt_bias), par(a_log), d_chan, spread2, select3)


def _ssm_step_kernel(xbc_ref, cst_ref, dt_ref, w_ref, b_ref, dtb_ref, alog_ref, dsk_ref, *refs):
    st_refs, (y_ref, sto_ref) = refs[:SSM_GROUPS], refs[SSM_GROUPS:]
    w = w_ref[...]
    acc = b_ref[...] + w[SSM_CONV_W - 1:SSM_CONV_W, :] * xbc_ref[0]
    for k in range(SSM_CONV_W - 1):
        acc = acc + w[k:k + 1, :] * cst_ref[0, k:k + 1, :]
    act = _silu(acc)
    nb = D_INNER
    row0 = lax.broadcasted_iota(jnp.int32, (2 * SUBLANES, 1), 0) == 0

    def dot3(a, b, dims):
        a_hi, a_mid, _ = _split3(a)
        b_hi, b_mid, _ = _split3(b)
        dg = lambda u, v: lax.dot_general(u, v, (dims, ((), ())), preferred_element_type=F32)
        return dg(a_hi, b_hi) + dg(a_hi, b_mid) + dg(a_mid, b_hi)

    for g in range(SSM_GROUPS):
        xs = act[:, g * GROUP_X:(g + 1) * GROUP_X]
        bm = act[:, nb + g * SSM_D_STATE:nb + (g + 1) * SSM_D_STATE]
        cm = act[:, nb + (SSM_GROUPS + g) * SSM_D_STATE:nb + (SSM_GROUPS + g + 1) * SSM_D_STATE]
        dt = jax.nn.softplus(dt_ref[0, :, g * LANES:(g + 1) * LANES] + dtb_ref[g])
        decay = jnp.exp(dt * (-jnp.exp(alog_ref[g])))
        xdt = jnp.concatenate(
            [xs[:, e * SSM_HEAD_DIM:(e + 1) * SSM_HEAD_DIM] * dt[:, e:e + 1] for e in range(HEADS_PER_GROUP)], axis=1)
        x8 = jnp.where(row0, xdt, 0.0)
        b8 = jnp.where(row0, bm, 0.0)
        c8 = jnp.where(row0, cm, 0.0)
        outer = dot3(x8, b8, ((0,), (0,)))
        new = []
        for e in range(HEADS_PER_GROUP):
            head = g * HEADS_PER_GROUP + e
            h_new = st_refs[g][0, e] * decay[:, e:e + 1] + outer[e * SSM_HEAD_DIM:(e + 1) * SSM_HEAD_DIM, :]
            sto_ref[0, head] = h_new
            new.append(h_new)
        h_g = jnp.concatenate(new, axis=0)
        y8 = dot3(c8, h_g, ((1,), (1,)))
        dsk = dsk_ref[g]
        dx = jnp.concatenate(
            [xs[:, e * SSM_HEAD_DIM:(e + 1) * SSM_HEAD_DIM] * dsk[:, e:e + 1] for e in range(HEADS_PER_GROUP)], axis=1)
        y_ref[0, :, g * GROUP_X:(g + 1) * GROUP_X] = y8[0:1, :] + dx


def _ssm_step(xbc, dt, conv_state, ssm_states, layer, conv_w, conv_b, dt_bias, a_log, d_skip):
    n = xbc.shape[0]
    par = _group_lanes
    st_spec = pl.BlockSpec((1, SSM_HEADS, SSM_HEAD_DIM, SSM_D_STATE), lambda s: (s, 0, 0, 0))
    st_in_specs = [pl.BlockSpec((None, 1, HEADS_PER_GROUP, SSM_HEAD_DIM, SSM_D_STATE),
                                lambda s, g=g: (layer, s, g, 0, 0)) for g in range(SSM_GROUPS)]
    y, st = pl.pallas_call(
        _ssm_step_kernel,
        out_shape=[jax.ShapeDtypeStruct((n, 1, D_INNER), F32),
                   jax.ShapeDtypeStruct(ssm_states.shape[1:], F32)],
        grid=(n,),
        in_specs=[pl.BlockSpec((1, 1, SSM_CONV_DIM), lambda s: (s, 0, 0)),
                  pl.BlockSpec((1, SSM_CONV_W - 1, SSM_CONV_DIM), lambda s: (s, 0, 0)),
                  pl.BlockSpec((1, 1, SSM_GROUPS * LANES), lambda s: (s, 0, 0)),
                  _resident((SSM_CONV_W, SSM_CONV_DIM)),
                  _resident((1, SSM_CONV_DIM)),
                  _resident((SSM_GROUPS, 1, LANES)),
                  _resident((SSM_GROUPS, 1, LANES)),
                  _resident((SSM_GROUPS, 1, LANES)),
                  *st_in_specs],
        out_specs=[pl.BlockSpec((1, 1, D_INNER), lambda s: (s, 0, 0)), st_spec],
        compiler_params=_cparams("parallel"),
        name="ssm_step",
    )(xbc.reshape(n, 1, SSM_CONV_DIM), conv_state, dt.reshape(n, 1, SSM_GROUPS * LANES), conv_w,
      conv_b.reshape(1, SSM_CONV_DIM), par(dt_bias), par(a_log), par(d_skip), *([ssm_states] * SSM_GROUPS))
    return y.reshape(n, D_INNER), st


MOBA_HEADS_PER_STEP = 4


def _moba_kernel(q_ref, k_ref, vt_ref, mean_ref, o_ref, bias_ref):
    i = pl.program_id(2)
    blk = MOBA_BLOCK
    n_blk = mean_ref.shape[1]
    n_heads = q_ref.shape[2] // HEAD_DIM
    sub = lax.broadcasted_iota(jnp.int32, (n_blk, blk), 0)
    sub_f = sub.astype(F32)
    key = lax.broadcasted_iota(jnp.int32, (blk, blk), 0)
    qry = lax.broadcasted_iota(jnp.int32, (blk, blk), 1)
    own = pl.multiple_of(i * blk, blk)
    heads = [slice(hd * HEAD_DIM, (hd + 1) * HEAD_DIM) for hd in range(n_heads)]

    qs_t = []
    for hd, lanes in enumerate(heads):
        q_t = q_ref[0, :, lanes].T
        gate = jnp.dot(mean_ref[0, :, lanes], q_t, precision=lax.Precision.HIGHEST,
                       preferred_element_type=F32)
        past = sub < i
        gate = jnp.where(past, gate, NEG_INF)
        bias = jnp.full((n_blk, blk), NEG_INF, F32)
        for _ in range(MOBA_TOPK):
            top = jnp.max(gate, axis=0, keepdims=True)
            first = jnp.min(jnp.where(gate == top, sub_f, float(n_blk)), axis=0, keepdims=True)
            pick = (sub_f == first) & past
            bias = jnp.where(pick, 0.0, bias)
            gate = jnp.where(pick, NEG_INF, gate)
        for j in range(n_blk):
            bias_ref[hd, j] = bias[j:j + 1, :]
        qs_t.append((q_t * (HEAD_DIM ** -0.5)).astype(BF16))

    def pair_scores(hd, j0):
        off = j0 * blk if isinstance(j0, int) else pl.multiple_of(j0 * blk, 2 * blk)
        s = _dot(k_ref[0, pl.ds(off, 2 * blk), heads[hd]], qs_t[hd])
        return s[:blk] + bias_ref[hd, j0], s[blk:] + bias_ref[hd, j0 + 1]

    s_own = [_dot(k_ref[0, pl.ds(own, blk), lanes], qs_t[hd]) for hd, lanes in enumerate(heads)]
    s_pair = [pair_scores(hd, 0) for hd in range(n_heads)]
    carry = []
    for hd, lanes in enumerate(heads):
        s = jnp.where(key <= qry, s_own[hd], NEG_INF)
        m = jnp.max(s, axis=0, keepdims=True)
        p = jnp.exp(s - m)
        l = jnp.sum(p, axis=0, keepdims=True)
        acc = _dot(vt_ref[0, i, lanes, :], p.astype(BF16))
        carry.append((m, l, acc) + s_pair[hd])

    def body(jj, carry):
        j0 = 2 * jj
        j_next = jnp.minimum(j0 + 2, n_blk - 2)
        s_next = [pair_scores(hd, j_next) for hd in range(n_heads)]
        out = []
        for hd, lanes in enumerate(heads):
            m, l, acc, s0, s1 = carry[hd]
            m_new = jnp.maximum(m, jnp.max(jnp.maximum(s0, s1), axis=0, keepdims=True))
            alpha = jnp.exp(m - m_new)
            p0 = jnp.exp(s0 - m_new)
            p1 = jnp.exp(s1 - m_new)
            l = alpha * l + jnp.sum(p0 + p1, axis=0, keepdims=True)
            v_pair = jnp.concatenate([vt_ref[0, j0, lanes, :], vt_ref[0, j0 + 1, lanes, :]], axis=1)
            p_pair = jnp.concatenate([p0.astype(BF16), p1.astype(BF16)], axis=0)
            acc = alpha * acc + _dot(v_pair, p_pair)
            out.append((m_new, l, acc) + s_next[hd])
        return tuple(out)

    carry = lax.fori_loop(0, (i + 1) // 2, body, tuple(carry))
    for hd, lanes in enumerate(heads):
        m, l, acc = carry[hd][:3]
        o_ref[0, :, lanes] = (acc / l).T.astype(o_ref.dtype)


def _moba_prompt(q, k16, v_t, kmean, n_seq, seq):
    n_blk = seq // MOBA_BLOCK
    width = MOBA_HEADS_PER_STEP * HEAD_DIM
    tile = pl.BlockSpec((1, MOBA_BLOCK, width), lambda n, h, i: (n, i, h))
    return pl.pallas_call(
        _moba_kernel,
        out_shape=jax.ShapeDtypeStruct((n_seq, seq, D_MODEL), BF16),
        grid=(n_seq, N_HEADS // MOBA_HEADS_PER_STEP, n_blk),
        in_specs=[tile,
                  pl.BlockSpec((1, seq, width), lambda n, h, i: (n, 0, h)),
                  pl.BlockSpec((1, n_blk, width, MOBA_BLOCK), lambda n, h, i: (n, 0, h, 0)),
                  pl.BlockSpec((1, n_blk, width), lambda n, h, i: (n, 0, h))],
        out_specs=tile,
        scratch_shapes=[pltpu.VMEM((MOBA_HEADS_PER_STEP, n_blk, 1, MOBA_BLOCK), F32)],
        compiler_params=_cparams("parallel", "parallel", "arbitrary"),
        name="moba_prompt",
    )(q, k16, v_t, kmean)


MEAN_BLOCKS_PER_STEP = 8


def _page_mean_kernel(pt_ref, *refs):
    del pt_ref
    o_ref = refs[-1]
    for b in range(MEAN_BLOCKS_PER_STEP):
        total = jnp.sum(refs[PAGES_PER_BLOCK * b][0], axis=0)
        for t in range(1, PAGES_PER_BLOCK):
            total = total + jnp.sum(refs[PAGES_PER_BLOCK * b + t][0], axis=0)
        o_ref[0, b] = total * (1.0 / MOBA_BLOCK)


def _page_means(cache_k, page_table):
    n_seq, n_pages = page_table.shape
    n_blk = n_pages // PAGES_PER_BLOCK
    per_step = MEAN_BLOCKS_PER_STEP * PAGES_PER_BLOCK
    page = lambda t: pl.BlockSpec(
        (1, PAGE_SIZE, N_HEADS, HEAD_DIM), lambda s, b, pt: (pt[s * n_pages + per_step * b + t], 0, 0, 0))
    out = pl.pallas_call(
        _page_mean_kernel,
        out_shape=jax.ShapeDtypeStruct((n_seq, n_blk, N_HEADS, HEAD_DIM), F32),
        grid_spec=pltpu.PrefetchScalarGridSpec(
            num_scalar_prefetch=1,
            grid=(n_seq, n_blk // MEAN_BLOCKS_PER_STEP),
            in_specs=[page(t) for t in range(per_step)],
            out_specs=pl.BlockSpec((1, MEAN_BLOCKS_PER_STEP, N_HEADS, HEAD_DIM), lambda s, b, pt: (s, b, 0, 0)),
        ),
        compiler_params=_cparams("parallel", "parallel"),
        name="page_means",
    )(page_table.reshape(-1), *([cache_k] * per_step))
    return out.reshape(n_seq, n_blk, D_MODEL)


def _select_kernel(q_ref, mean_ref, idx_ref):
    n_blk = mean_ref.shape[1]
    prod = mean_ref[0] * q_ref[0]
    lane = lax.broadcasted_iota(jnp.int32, (n_blk, LANES), 1)
    sub = lax.broadcasted_iota(jnp.int32, (n_blk, LANES), 0)
    gate = jnp.zeros((n_blk, LANES), F32)
    for hd in range(N_HEADS):
        g_h = jnp.sum(prod[:, hd * HEAD_DIM:(hd + 1) * HEAD_DIM], axis=1, keepdims=True)
        gate = jnp.where(lane == hd, g_h, gate)
    out_sub = lax.broadcasted_iota(jnp.int32, (SUBLANES, LANES), 0)
    out = jnp.zeros((SUBLANES, LANES), jnp.int32)
    for j in range(n_blk):
        g_j = gate[j:j + 1, :]
        beats = (gate > g_j) | ((gate == g_j) & (sub < j))
        rank = jnp.sum(jnp.where(beats, 1.0, 0.0), axis=0, keepdims=True)
        for r in range(MOBA_TOPK):
            out = jnp.where((out_sub == r) & (rank == float(r)), j, out)
    idx_ref[0] = out


def _select_blocks(q, means):
    n_seq, n_blk, _ = means.shape
    idx = pl.pallas_call(
        _select_kernel,
        out_shape=jax.ShapeDtypeStruct((n_seq, SUBLANES, LANES), jnp.int32),
        grid=(n_seq,),
        in_specs=[pl.BlockSpec((1, 1, D_MODEL), lambda s: (s, 0, 0)),
                  pl.BlockSpec((1, n_blk, D_MODEL), lambda s: (s, 0, 0))],
        out_specs=pl.BlockSpec((1, SUBLANES, LANES), lambda s: (s, 0, 0)),
        compiler_params=_cparams("parallel"),
        name="select_blocks",
    )(q.reshape(n_seq, 1, D_MODEL), means)
    return idx[:, :MOBA_TOPK, :N_HEADS]


N_SEL_PAGES = MOBA_TOPK * PAGES_PER_BLOCK


def _decode_attn_kernel(pt_ref, idx_ref, q_ref, kn_ref, vn_ref, k_hbm, v_hbm, o_ref, kbuf, vbuf, sems,
                        *, n_pages):
    s_i = pl.program_id(0)
    n_seq = pl.num_programs(0)
    buf = s_i % 2

    def copies(seq, hd, slot):
        r, t = divmod(slot, PAGES_PER_BLOCK)
        blk = idx_ref[(seq * MOBA_TOPK + r) * N_HEADS + hd]
        pg = pt_ref[seq * n_pages + PAGES_PER_BLOCK * blk + t]
        half = seq % 2
        return (pltpu.make_async_copy(k_hbm.at[pg, :, hd, :], kbuf.at[half, hd, slot], sems.at[half, 0, hd, slot]),
                pltpu.make_async_copy(v_hbm.at[pg, :, hd, :], vbuf.at[half, hd, slot], sems.at[half, 1, hd, slot]))

    def start_all(seq):
        for hd in range(N_HEADS):
            for slot in range(N_SEL_PAGES):
                for cp in copies(seq, hd, slot):
                    cp.start()

    @pl.when(s_i == 0)
    def _():
        start_all(s_i)

    @pl.when(s_i + 1 < n_seq)
    def _():
        start_all(s_i + 1)

    for hd in range(N_HEADS):
        lanes = slice(hd * HEAD_DIM, (hd + 1) * HEAD_DIM)
        for slot in range(N_SEL_PAGES):
            for cp in copies(s_i, hd, slot):
                cp.wait()
        q = q_ref[0, :, lanes] * (HEAD_DIM ** -0.5)
        s_own = jnp.sum(q * kn_ref[0, :, lanes], axis=1, keepdims=True)
        scores = [jnp.sum(kbuf[buf, hd, slot] * q, axis=1, keepdims=True) for slot in range(N_SEL_PAGES)]
        m = s_own
        for s in scores:
            m = jnp.maximum(m, jnp.max(s, axis=0, keepdims=True))
        p_own = jnp.exp(s_own - m)
        l = p_own
        acc = p_own * vn_ref[0, :, lanes]
        for slot, s in enumerate(scores):
            p = jnp.exp(s - m)
            l = l + jnp.sum(p, axis=0, keepdims=True)
            acc = acc + jnp.sum(p * vbuf[buf, hd, slot], axis=0, keepdims=True)
        o_ref[0, :, lanes] = acc / l


def _decode_attn(q, k_new, v_new, cache_k, cache_v, page_table, idx):
    n_seq, n_pages = page_table.shape
    vec = pl.BlockSpec((1, 1, D_MODEL), lambda s, pt, sel: (s, 0, 0))
    hbm = pl.BlockSpec(memory_space=pl.ANY)
    r3 = lambda a: a.reshape(n_seq, 1, D_MODEL)
    slabs = (2, N_HEADS, N_SEL_PAGES, PAGE_SIZE, HEAD_DIM)
    out = pl.pallas_call(
        functools.partial(_decode_attn_kernel, n_pages=n_pages),
        out_shape=jax.ShapeDtypeStruct((n_seq, 1, D_MODEL), F32),
        grid_spec=pltpu.PrefetchScalarGridSpec(
            num_scalar_prefetch=2,
            grid=(n_seq,),
            in_specs=[vec, vec, vec, hbm, hbm],
            out_specs=vec,
            scratch_shapes=[pltpu.VMEM(slabs, F32), pltpu.VMEM(slabs, F32),
                            pltpu.SemaphoreType.DMA((2, 2, N_HEADS, N_SEL_PAGES))],
        ),
        compiler_params=_cparams("arbitrary"),
        name="decode_attn",
    )(page_table.reshape(-1), idx.reshape(-1), r3(q), r3(k_new), r3(v_new), cache_k, cache_v)
    return out.reshape(n_seq, D_MODEL)


def _cast_kernel(*refs, n_parts):
    w_refs, o_refs = refs[:-n_parts], refs[-n_parts:]
    streams = len(w_refs) // n_parts
    rows = w_refs[0].shape[0]
    for p, o_ref in enumerate(o_refs):
        for r in range(streams):
            o_ref[r * rows:(r + 1) * rows, :] = w_refs[p * streams + r][...].astype(o_ref.dtype)


CAST_STREAMS = 4


def _to_bf16(w, layer, row_tile, col_parts=1):
    _, rows, cols = w.shape
    streams = CAST_STREAMS if rows % (CAST_STREAMS * row_tile) == 0 else 1
    width = cols // col_parts
    in_specs = [pl.BlockSpec((None, row_tile, width), lambda i, p=p, r=r: (layer, i * streams + r, p))
                for p in range(col_parts) for r in range(streams)]
    out = pl.pallas_call(
        functools.partial(_cast_kernel, n_parts=col_parts),
        out_shape=[jax.ShapeDtypeStruct((rows, width), BF16)] * col_parts,
        grid=(rows // (streams * row_tile),),
        in_specs=in_specs,
        out_specs=[pl.BlockSpec((streams * row_tile, width), lambda i: (i, 0))] * col_parts,
        compiler_params=_cparams("parallel"),
        name="to_bf16",
    )(*([w] * len(in_specs)))
    return out[0] if col_parts == 1 else tuple(out)


def _dt_weight(w_in):
    w_dt = w_in[:, D_INNER + SSM_CONV_DIM:].reshape(D_MODEL, SSM_GROUPS, HEADS_PER_GROUP)
    w_dt = jnp.pad(w_dt, ((0, 0), (0, 0), (0, LANES - HEADS_PER_GROUP)))
    return _to_bf16(w_dt.reshape(1, D_MODEL, SSM_GROUPS * LANES), 0, D_MODEL)


W_ROW_TILE = 32
W_DOWN_ROW_TILE = D_FF // 16


def kernel(x_prompt, x_sample, state_conv, state_ssm, cache_k, cache_v, page_table, norm_ffn_a, w_ffn_a_up, w_ffn_a_down, norm_mix, norm_ffn_b, w_ffn_b_up, w_ffn_b_down, ssm_w_in, ssm_conv_w, ssm_conv_b, ssm_dt_bias, ssm_a_log, ssm_d, ssm_norm, ssm_w_out, norm_kv, w_kv, w_q, w_o, norm_final):
    n_p, seq, d = x_prompt.shape
    n_s = x_sample.shape[0]
    past_len = page_table.shape[1] * PAGE_SIZE
    tm_p = 512
    tm_s = n_s

    hp = x_prompt.reshape(n_p * seq, d)
    hs = x_sample.reshape(n_s, d)
    tab_p = _rope_tables(jnp.arange(seq, dtype=jnp.int32))
    tab_s = _rope_tables(jnp.full((n_s,), past_len, jnp.int32))

    conv_p, ssm_p, conv_s, ssm_s = [], [], [], []
    k_p = v_p = k_s = v_s = None
    for layer in range(DEPTH):
        if layer == N_A_LAYERS:
            w_kv16 = _to_bf16(w_kv[None], 0, W_ROW_TILE)
            k_p, v_p, kmean_p, k16, v_t = _kv_proj(hp, norm_kv, w_kv16, tab_p, MOBA_BLOCK, True)
            k_s, v_s = _kv_proj(hs, norm_kv, w_kv16, tab_s, tm_s, False)
            k16 = k16.reshape(n_p, seq, d)
            v_t = v_t.reshape(n_p, seq // MOBA_BLOCK, d, MOBA_BLOCK)
            kmean_p = kmean_p.reshape(n_p, seq // MOBA_BLOCK, d)
            kmean_s = _page_means(cache_k, page_table)

        w_up = _to_bf16(w_ffn_a_up, layer, W_ROW_TILE, 2)
        w_dn = _to_bf16(w_ffn_a_down, layer, W_DOWN_ROW_TILE, 2)
        hp = _ffn(hp, norm_ffn_a[layer], w_up, w_dn, tm_p)
        hs = _ffn(hs, norm_ffn_a[layer], w_up, w_dn, tm_s)

        if layer < N_A_LAYERS:
            w_in = _to_bf16(ssm_w_in, layer, W_ROW_TILE)
            w_dt = _dt_weight(ssm_w_in[layer])
            w_out = _to_bf16(ssm_w_out, layer, W_ROW_TILE)
            ssm_args = (ssm_conv_w[layer], ssm_conv_b[layer], ssm_dt_bias[layer], ssm_a_log[layer], ssm_d[layer])
            z, xbc, dt, tail = _in_proj(hp, norm_mix[layer], w_in, w_dt, ssm_conv_w[layer], ssm_conv_b[layer],
                                        tm_p, seq)
            y, st = _ssd_prompt(xbc, dt, *ssm_args[2:], n_p, seq)
            conv_p.append(tail[:, SUBLANES - (SSM_CONV_W - 1):])
            ssm_p.append(st)
            mix_p = (y.reshape(n_p * seq, D_INNER), w_out, z, ssm_norm[layer])

            z, xbc = _proj(hs, norm_mix[layer], w_in, (D_INNER, SSM_CONV_DIM), tm_s)
            dt, = _proj(hs, norm_mix[layer], w_dt, (SSM_GROUPS * LANES,), tm_s)
            y, st = _ssm_step(xbc, dt, state_conv[layer], state_ssm, layer, *ssm_args)
            conv_s.append(jnp.concatenate([state_conv[layer][:, 1:], xbc[:, None, :]], axis=1))
            ssm_s.append(st)
            mix_s = (y, w_out, z, ssm_norm[layer])
        else:
            j = layer - N_A_LAYERS
            w_q16, w_o16 = _to_bf16(w_q, j, W_ROW_TILE), _to_bf16(w_o, j, W_ROW_TILE)
            q = _q_proj(hp, norm_mix[layer], w_q16, tab_p, tm_p)
            o = _moba_prompt(q.reshape(n_p, seq, d), k16, v_t, kmean_p, n_p, seq)
            mix_p = (o.reshape(n_p * seq, d), w_o16)

            q = _q_proj(hs, norm_mix[layer], w_q16, tab_s, tm_s)
            idx = _select_blocks(q, kmean_s)
            o = _decode_attn(q, k_s, v_s, cache_k, cache_v, page_table, idx)
            mix_s = (o, w_o16)

        w_up = _to_bf16(w_ffn_b_up, layer, W_ROW_TILE, 2)
        w_dn = _to_bf16(w_ffn_b_down, layer, W_DOWN_ROW_TILE, 2)
        final_g = norm_final if layer == DEPTH - 1 else None
        hp = _ffn(hp, norm_ffn_b[layer], w_up, w_dn, tm_p, final_g, mix_p)
        hs = _ffn(hs, norm_ffn_b[layer], w_up, w_dn, tm_s, final_g, mix_s)

    heads = (N_HEADS, HEAD_DIM)
    return (hp.reshape(n_p, seq, d), hs.reshape(n_s, 1, d),
            jnp.stack(conv_p), jnp.stack(ssm_p),
            k_p.reshape(n_p, seq, *heads), v_p.reshape(n_p, seq, *heads),
            jnp.stack(conv_s), jnp.stack(ssm_s),
            k_s.reshape(n_s, 1, *heads), v_s.reshape(n_s, 1, *heads))
```

```python
import functools
import math

import jax
import jax.numpy as jnp
from jax import lax
from jax.experimental import pallas as pl
from jax.experimental.pallas import tpu as pltpu

F32 = jnp.float32
BF16 = jnp.bfloat16

D_MODEL = 1024
DEPTH = 4
N_A_LAYERS = 2
NORM_EPS = 1e-5
D_FF = 2816
D_INNER = 2048
SSM_HEAD_DIM = 64
SSM_HEADS = 32
SSM_GROUPS = 4
HEADS_PER_GROUP = 8
SSM_D_STATE = 128
SSM_CONV_W = 4
SSM_CONV_DIM = D_INNER + 2 * SSM_GROUPS * SSM_D_STATE
SSD_CHUNK = 128
GROUP_X = HEADS_PER_GROUP * SSM_HEAD_DIM
N_HEADS = 8
HEAD_DIM = 128
ROT_DIM = 32
ROPE_THETA = 500000.0
MOBA_BLOCK = 256
MOBA_TOPK = 3
PAGE_SIZE = 128
PAGES_PER_BLOCK = MOBA_BLOCK // PAGE_SIZE

LANES = 128
SUBLANES = 8
VMEM_LIMIT_BYTES = 56 * 1024 * 1024
FF_CHUNK = 256
NEG_INF = float("-inf")


def _cparams(*sem):
    return pltpu.CompilerParams(dimension_semantics=sem, vmem_limit_bytes=VMEM_LIMIT_BYTES)


def _resident(shape):
    nd = len(shape)
    return pl.BlockSpec(shape, lambda *_: (0,) * nd, pipeline_mode=pl.Buffered(1))


def _rms(x, g):
    return x * lax.rsqrt(jnp.mean(x * x, axis=-1, keepdims=True) + NORM_EPS) * g


def _silu(x):
    return x * jax.nn.sigmoid(x)


def _split3(x):
    hi = x.astype(BF16)
    r1 = x - hi.astype(F32)
    mid = r1.astype(BF16)
    lo = (r1 - mid.astype(F32)).astype(BF16)
    return hi, mid, lo


def _dot(a, b):
    return jnp.dot(a, b, preferred_element_type=F32)


def _dot_nt(a, b):
    return lax.dot_general(a, b, (((1,), (1,)), ((), ())), preferred_element_type=F32)


def _ffn_kernel(*refs, mixer, final_norm):
    refs = list(refs)
    o_ref = refs.pop()
    gf_ref = refs.pop() if final_norm else None
    x_ref, g_ref, wg_ref, wu_ref, wda_ref, wdb_ref = refs[:6]
    x = x_ref[...]
    if mixer == "gated":
        y_ref, wo_ref, z_ref, nw_ref = refs[6:]
        y = _rms(y_ref[...].astype(F32) * _silu(z_ref[...].astype(F32)), nw_ref[...])
        x = x + _dot(y.astype(BF16), wo_ref[...])
    elif mixer == "plain":
        y_ref, wo_ref = refs[6:]
        x = x + _dot(y_ref[...].astype(BF16), wo_ref[...])
    xn = _rms(x, g_ref[...]).astype(BF16)
    acc = jnp.zeros_like(x)
    for c in range(D_FF // FF_CHUNK):
        lo, hi = c * FF_CHUNK, (c + 1) * FF_CHUNK
        gate = _dot(xn, wg_ref[:, lo:hi])
        up = _dot(xn, wu_ref[:, lo:hi])
        act = (_silu(gate) * up).astype(BF16)
        acc = acc + jnp.concatenate([_dot(act, wda_ref[lo:hi, :]), _dot(act, wdb_ref[lo:hi, :])], axis=1)
    h = x + 0.5 * acc
    if final_norm:
        h = _rms(h, gf_ref[...])
    o_ref[...] = h


def _ffn(h, g, w_up, w_dn, tm, final_g=None, mix=None):
    t, d = h.shape
    row = pl.BlockSpec((tm, d), lambda i: (i, 0))
    w_gate, w_upper = w_up
    w_dn_a, w_dn_b = w_dn
    weights = [w_gate, w_upper, w_dn_a, w_dn_b]
    in_specs = [row, _resident((1, d))] + [_resident(w.shape) for w in weights]
    args = [h, g.reshape(1, d)] + weights
    mixer = None
    if mix is not None:
        y, w_o = mix[:2]
        k = y.shape[1]
        row_k = pl.BlockSpec((tm, k), lambda i: (i, 0))
        in_specs += [row_k, _resident(w_o.shape)]
        args += [y, w_o]
        mixer = "plain"
        if len(mix) == 4:
            in_specs += [row_k, _resident((1, k))]
            args += [mix[2], mix[3].reshape(1, k)]
            mixer = "gated"
    if final_g is not None:
        in_specs.append(_resident((1, d)))
        args.append(final_g.reshape(1, d))
    return pl.pallas_call(
        functools.partial(_ffn_kernel, mixer=mixer, final_norm=final_g is not None),
        out_shape=jax.ShapeDtypeStruct((t, d), F32),
        grid=(t // tm,),
        in_specs=in_specs,
        out_specs=row,
        compiler_params=_cparams("parallel"),
        name="ffn",
    )(*args)


PROJ_CHUNK = 512


def _store_proj(xn, w_ref, o_ref, col0, width):
    for c0 in range(0, width, PROJ_CHUNK):
        c1 = min(c0 + PROJ_CHUNK, width)
        o_ref[:, c0:c1] = _dot(xn, w_ref[:, col0 + c0:col0 + c1]).astype(o_ref.dtype)


def _proj_kernel(x_ref, g_ref, w_ref, *o_refs):
    xn = _rms(x_ref[...], g_ref[...]).astype(BF16)
    col = 0
    for o_ref in o_refs:
        width = o_ref.shape[1]
        _store_proj(xn, w_ref, o_ref, col, width)
        col += width


def _proj(h, g, w, widths, tm):
    t, d = h.shape
    return pl.pallas_call(
        _proj_kernel,
        out_shape=[jax.ShapeDtypeStruct((t, n), F32) for n in widths],
        grid=(t // tm,),
        in_specs=[pl.BlockSpec((tm, d), lambda i: (i, 0)), _resident((1, d)), _resident(w.shape)],
        out_specs=[pl.BlockSpec((tm, n), lambda i: (i, 0)) for n in widths],
        compiler_params=_cparams("parallel"),
        name="proj",
    )(h, g.reshape(1, d), w)


def _rope(y, cos, sin_lo, sin_hi):
    outs = []
    for hd in range(y.shape[1] // HEAD_DIM):
        xh = y[:, hd * HEAD_DIM:(hd + 1) * HEAD_DIM]
        up = pltpu.roll(xh, HEAD_DIM - ROT_DIM // 2, axis=1)
        dn = pltpu.roll(xh, ROT_DIM // 2, axis=1)
        outs.append(xh * cos + up * sin_lo + dn * sin_hi)
    return jnp.concatenate(outs, axis=1)


def _rope_tables(pos):
    half = ROT_DIM // 2
    inv = ROPE_THETA ** (-2.0 * jnp.arange(half, dtype=F32) / ROT_DIM)
    ang = pos.astype(F32)[:, None] * inv[None, :]
    cos, sin = jnp.cos(ang), jnp.sin(ang)
    n = pos.shape[0]
    ones = jnp.ones((n, HEAD_DIM - ROT_DIM), F32)
    zeros = jnp.zeros((n, HEAD_DIM - half), F32)
    cos_t = jnp.concatenate([cos, cos, ones], axis=1)
    sin_lo = jnp.concatenate([-sin, zeros], axis=1)
    sin_hi = jnp.concatenate([jnp.zeros((n, half), F32), sin, jnp.zeros((n, HEAD_DIM - ROT_DIM), F32)], axis=1)
    return cos_t, sin_lo, sin_hi


def _q_kernel(x_ref, g_ref, w_ref, cos_ref, slo_ref, shi_ref, q_ref):
    xn = _rms(x_ref[...], g_ref[...]).astype(BF16)
    q_ref[...] = _rope(_dot(xn, w_ref[...]), cos_ref[...], slo_ref[...], shi_ref[...])


def _q_proj(h, g, w, tables, tm):
    t, d = h.shape
    n_tab = tables[0].shape[0] // tm
    tab = pl.BlockSpec((tm, HEAD_DIM), lambda i: (i % n_tab, 0))
    row = pl.BlockSpec((tm, d), lambda i: (i, 0))
    return pl.pallas_call(
        _q_kernel,
        out_shape=jax.ShapeDtypeStruct((t, d), F32),
        grid=(t // tm,),
        in_specs=[row, _resident((1, d)), _resident(w.shape), tab, tab, tab],
        out_specs=row,
        compiler_params=_cparams("parallel"),
        name="q_proj",
    )(h, g.reshape(1, d), w, *tables)


def _kv_kernel(x_ref, g_ref, w_ref, cos_ref, slo_ref, shi_ref, k_ref, v_ref, *block_refs, block_sums):
    xn = _rms(x_ref[...], g_ref[...]).astype(BF16)
    k = _rope(_dot(xn, w_ref[:, :D_MODEL]), cos_ref[...], slo_ref[...], shi_ref[...])
    v = _dot(xn, w_ref[:, D_MODEL:])
    k_ref[...] = k
    v_ref[...] = v
    if block_sums:
        ksum_ref, k16_ref, vt16_ref = block_refs
        ksum_ref[0] = jnp.sum(k, axis=0, keepdims=True) * (1.0 / MOBA_BLOCK)
        k16_ref[...] = k.astype(BF16)
        vt16_ref[0] = v.T.astype(BF16)


def _kv_proj(h, g, w, tables, tm, block_sums):
    t, d = h.shape
    n_tab = tables[0].shape[0] // tm
    tab = pl.BlockSpec((tm, HEAD_DIM), lambda i: (i % n_tab, 0))
    row = pl.BlockSpec((tm, d), lambda i: (i, 0))
    out_shape = [jax.ShapeDtypeStruct((t, d), F32), jax.ShapeDtypeStruct((t, d), F32)]
    out_specs = [row, row]
    if block_sums:
        assert tm == MOBA_BLOCK
        out_shape += [jax.ShapeDtypeStruct((t // tm, 1, d), F32), jax.ShapeDtypeStruct((t, d), BF16),
                      jax.ShapeDtypeStruct((t // tm, d, tm), BF16)]
        out_specs += [pl.BlockSpec((1, 1, d), lambda i: (i, 0, 0)), row,
                      pl.BlockSpec((1, d, tm), lambda i: (i, 0, 0))]
    return pl.pallas_call(
        functools.partial(_kv_kernel, block_sums=block_sums),
        out_shape=out_shape,
        grid=(t // tm,),
        in_specs=[row, _resident((1, d)), _resident(w.shape), tab, tab, tab],
        out_specs=out_specs,
        compiler_params=_cparams("parallel"),
        name="kv_proj",
    )(h, g.reshape(1, d), w, *tables)


def _in_proj_kernel(x_ref, g_ref, w_ref, wdt_ref, cw_ref, cb_ref, z_ref, xbc_ref, dt_ref, tail_ref, ext_ref,
                    *, tiles_per_seq):
    tm = x_ref.shape[0]
    xn = _rms(x_ref[...], g_ref[...]).astype(BF16)

    @pl.when(pl.program_id(0) % tiles_per_seq == 0)
    def _():
        ext_ref[...] = jnp.zeros((SUBLANES, SSM_CONV_DIM), F32)

    first_rows = lax.broadcasted_iota(jnp.int32, (SUBLANES, PROJ_CHUNK), 0)

    def xbc_chunk(c0):
        return _dot(xn, w_ref[:, D_INNER + c0:D_INNER + c0 + PROJ_CHUNK])

    other = [(z_ref, c0, w_ref) for c0 in range(0, D_INNER, PROJ_CHUNK)]
    other.append((dt_ref, 0, wdt_ref))
    pending = xbc_chunk(0)
    for c0 in range(0, SSM_CONV_DIM, PROJ_CHUNK):
        cols = slice(c0, c0 + PROJ_CHUNK)
        raw = pending
        if c0 + PROJ_CHUNK < SSM_CONV_DIM:
            pending = xbc_chunk(c0 + PROJ_CHUNK)
        if other:
            o_ref, dst, src_ref = other.pop(0)
            o_ref[:, dst:dst + PROJ_CHUNK] = _dot(xn, src_ref[:, dst:dst + PROJ_CHUNK]).astype(o_ref.dtype)
        prev = ext_ref[:, cols]
        acc = cb_ref[:, cols] + cw_ref[SSM_CONV_W - 1:SSM_CONV_W, cols] * raw
        for back in range(1, SSM_CONV_W):
            cur = pltpu.roll(raw, back, axis=0)
            head = jnp.where(first_rows < back, pltpu.roll(prev, back, axis=0), cur[:SUBLANES])
            shifted = jnp.concatenate([head, cur[SUBLANES:]], axis=0)
            acc = acc + cw_ref[SSM_CONV_W - 1 - back:SSM_CONV_W - back, cols] * shifted
        xbc_ref[:, cols] = _silu(acc).astype(xbc_ref.dtype)
        tail = raw[tm - SUBLANES:]
        tail_ref[0, :, cols] = tail
        ext_ref[:, cols] = tail


def _in_proj(h, g, w, w_dt, conv_w, conv_b, tm, seq):
    t, d = h.shape
    tiles_per_seq = seq // tm
    row = lambda n: pl.BlockSpec((tm, n), lambda i: (i, 0))
    return pl.pallas_call(
        functools.partial(_in_proj_kernel, tiles_per_seq=tiles_per_seq),
        out_shape=[jax.ShapeDtypeStruct((t, D_INNER), BF16),
                   jax.ShapeDtypeStruct((t, SSM_CONV_DIM), BF16),
                   jax.ShapeDtypeStruct((t, SSM_GROUPS * LANES), F32),
                   jax.ShapeDtypeStruct((t // seq, SUBLANES, SSM_CONV_DIM), F32)],
        grid=(t // tm,),
        in_specs=[row(d), _resident((1, d)), _resident(w.shape), _resident(w_dt.shape),
                  _resident(conv_w.shape), _resident((1, SSM_CONV_DIM))],
        out_specs=[row(D_INNER), row(SSM_CONV_DIM), row(SSM_GROUPS * LANES),
                   pl.BlockSpec((1, SUBLANES, SSM_CONV_DIM), lambda i: (i // tiles_per_seq, 0, 0))],
        scratch_shapes=[pltpu.VMEM((SUBLANES, SSM_CONV_DIM), F32)],
        compiler_params=_cparams("arbitrary"),
        name="in_proj",
    )(h, g.reshape(1, d), w, w_dt, conv_w, conv_b.reshape(1, SSM_CONV_DIM))


def _split_dot(x, w, terms):
    return _dot(jnp.concatenate(_split3(x)[:terms], axis=1), w)


SSD_GROUPS_PER_STEP = 4


def _ssd_kernel(x_ref, b_ref, c_ref, dt_ref, dtb_ref, alog_ref, dsk_ref, spread_ref, y_ref, st_ref):
    cs = SSD_CHUNK

    @pl.when(pl.program_id(2) == 0)
    def _():
        st_ref[...] = jnp.zeros(st_ref.shape, F32)

    row = lax.broadcasted_iota(jnp.int32, (cs, cs), 0)
    col = lax.broadcasted_iota(jnp.int32, (cs, cs), 1)
    causal = row >= col
    tril = jnp.where(causal, 1.0, 0.0).astype(BF16)
    low_half = col < SSM_HEAD_DIM
    spread = spread_ref[...]

    for g in range(SSD_GROUPS_PER_STEP):
        heads = slice(g * HEADS_PER_GROUP, (g + 1) * HEADS_PER_GROUP)
        xs = x_ref[0, :, g * GROUP_X:(g + 1) * GROUP_X].astype(F32)
        bm16 = b_ref[0, :, g * SSM_D_STATE:(g + 1) * SSM_D_STATE]
        cm16 = c_ref[0, :, g * SSM_D_STATE:(g + 1) * SSM_D_STATE]

        dt = jax.nn.softplus(dt_ref[:, g * LANES:(g + 1) * LANES] + dtb_ref[g])
        a = jnp.where(col < HEADS_PER_GROUP, dt * (-jnp.exp(alog_ref[g])), 0.0)
        a_hi, a_mid, a_lo = _split3(a)
        acum = _dot(tril, a_hi) + _dot(tril, a_mid) + _dot(tril, a_lo)
        acum_t = acum.T
        ac_last = acum[cs - 1:cs, :]

        xdt = xs * _split_dot(dt, spread, 2)
        xdt16 = xdt.astype(BF16)
        xw = xdt * _split_dot(jnp.exp(ac_last - acum), spread, 2)

        cb = _dot_nt(cm16, bm16)
        st_old = st_ref[0, heads].reshape(GROUP_X, SSM_D_STATE)
        y = _dot_nt(cm16, st_old.astype(BF16)) * _split_dot(jnp.exp(acum), spread, 2) + dsk_ref[g] * xs

        for pair in range(HEADS_PER_GROUP // 2):
            x_pair = xdt16[:, pair * LANES:(pair + 1) * LANES]
            y_pair = None
            for half in range(2):
                e = 2 * pair + half
                seg = acum[:, e:e + 1] - acum_t[e:e + 1, :]
                lmat = jnp.exp(jnp.where(causal, seg, NEG_INF))
                x_half = jnp.where(low_half if half == 0 else ~low_half, x_pair, jnp.zeros_like(x_pair))
                part = _dot((cb * lmat).astype(BF16), x_half)
                y_pair = part if y_pair is None else y_pair + part
            lanes = slice(g * GROUP_X + pair * LANES, g * GROUP_X + (pair + 1) * LANES)
            y_ref[0, :, lanes] = (y[:, pair * LANES:(pair + 1) * LANES] + y_pair).astype(y_ref.dtype)

        states = _dot(xw.T.astype(BF16), bm16)
        for e in range(HEADS_PER_GROUP):
            lo, hi = e * SSM_HEAD_DIM, (e + 1) * SSM_HEAD_DIM
            head = g * HEADS_PER_GROUP + e
            st_ref[0, head] = st_ref[0, head] * jnp.exp(ac_last[:, e:e + 1]) + states[lo:hi, :]


def _group_lanes(v):
    v = v.astype(F32).reshape(SSM_GROUPS, 1, HEADS_PER_GROUP)
    return jnp.pad(v, ((0, 0), (0, 0), (0, LANES - HEADS_PER_GROUP)))


def _ssd_prompt(xbc, dt, dt_bias, a_log, d_skip, n_seq, seq):
    nc = seq // SSD_CHUNK
    cs = SSD_CHUNK
    xbc = xbc.reshape(n_seq, seq, SSM_CONV_DIM)
    nb = D_INNER // SSM_D_STATE
    par = _group_lanes
    src = jnp.arange(LANES, dtype=jnp.int32)[:, None]
    spread = (jnp.arange(GROUP_X, dtype=jnp.int32)[None, :] // SSM_HEAD_DIM == src).astype(BF16)
    spread2 = jnp.tile(spread, (2, 1))
    d_chan = jnp.repeat(d_skip.astype(F32), SSM_HEAD_DIM).reshape(SSM_GROUPS, 1, GROUP_X)
    gps = SSD_GROUPS_PER_STEP
    b_blk = nb // gps
    c_blk = (nb + SSM_GROUPS) // gps

    def lanes(width, idx):
        return pl.BlockSpec((1, cs, gps * width), idx)

    def per_group(width):
        return pl.BlockSpec((gps, 1, width), lambda n, g, c: (g, 0, 0))

    in_specs = [
        lanes(GROUP_X, lambda n, g, c: (n, c, g)),
        lanes(SSM_D_STATE, lambda n, g, c: (n, c, b_blk + g)),
        lanes(SSM_D_STATE, lambda n, g, c: (n, c, c_blk + g)),
        pl.BlockSpec((cs, gps * LANES), lambda n, g, c: (n * nc + c, g)),
        per_group(LANES),
        per_group(LANES),
        per_group(GROUP_X),
        _resident(spread2.shape),
    ]
    return pl.pallas_call(
        _ssd_kernel,
        out_shape=[jax.ShapeDtypeStruct((n_seq, seq, D_INNER), BF16),
                   jax.ShapeDtypeStruct((n_seq, SSM_HEADS, SSM_HEAD_DIM, SSM_D_STATE), F32)],
        grid=(n_seq, SSM_GROUPS // gps, nc),
        in_specs=in_specs,
        out_specs=[lanes(GROUP_X, lambda n, g, c: (n, c, g)),
                   pl.BlockSpec((1, gps * HEADS_PER_GROUP, SSM_HEAD_DIM, SSM_D_STATE),
                                lambda n, g, c: (n, g, 0, 0))],
        compiler_params=_cparams("parallel", "parallel", "arbitrary"),
        name="ssd_prompt",
    )(xbc, xbc, xbc, dt, par(dt_bias), par(a_log), d_chan, spread2)


def _ssm_step_kernel(xbc_ref, cst_ref, dt_ref, w_ref, b_ref, dtb_ref, alog_ref, dsk_ref, *refs):
    st_refs, (y_ref, sto_ref) = refs[:SSM_GROUPS], refs[SSM_GROUPS:]
    w = w_ref[...]
    acc = b_ref[...] + w[SSM_CONV_W - 1:SSM_CONV_W, :] * xbc_ref[0]
    for k in range(SSM_CONV_W - 1):
        acc = acc + w[k:k + 1, :] * cst_ref[0, k:k + 1, :]
    act = _silu(acc)
    nb = D_INNER
    row0 = lax.broadcasted_iota(jnp.int32, (2 * SUBLANES, 1), 0) == 0

    def dot3(a, b, dims):
        a_hi, a_mid, _ = _split3(a)
        b_hi, b_mid, _ = _split3(b)
        dg = lambda u, v: lax.dot_general(u, v, (dims, ((), ())), preferred_element_type=F32)
        return dg(a_hi, b_hi) + dg(a_hi, b_mid) + dg(a_mid, b_hi)

    for g in range(SSM_GROUPS):
        xs = act[:, g * GROUP_X:(g + 1) * GROUP_X]
        bm = act[:, nb + g * SSM_D_STATE:nb + (g + 1) * SSM_D_STATE]
        cm = act[:, nb + (SSM_GROUPS + g) * SSM_D_STATE:nb + (SSM_GROUPS + g + 1) * SSM_D_STATE]
        dt = jax.nn.softplus(dt_ref[0, :, g * LANES:(g + 1) * LANES] + dtb_ref[g])
        decay = jnp.exp(dt * (-jnp.exp(alog_ref[g])))
        xdt = jnp.concatenate(
            [xs[:, e * SSM_HEAD_DIM:(e + 1) * SSM_HEAD_DIM] * dt[:, e:e + 1] for e in range(HEADS_PER_GROUP)], axis=1)
        x8 = jnp.where(row0, xdt, 0.0)
        b8 = jnp.where(row0, bm, 0.0)
        c8 = jnp.where(row0, cm, 0.0)
        outer = dot3(x8, b8, ((0,), (0,)))
        new = []
        for e in range(HEADS_PER_GROUP):
            head = g * HEADS_PER_GROUP + e
            h_new = st_refs[g][0, e] * decay[:, e:e + 1] + outer[e * SSM_HEAD_DIM:(e + 1) * SSM_HEAD_DIM, :]
            sto_ref[0, head] = h_new
            new.append(h_new)
        h_g = jnp.concatenate(new, axis=0)
        y8 = dot3(c8, h_g, ((1,), (1,)))
        dsk = dsk_ref[g]
        dx = jnp.concatenate(
            [xs[:, e * SSM_HEAD_DIM:(e + 1) * SSM_HEAD_DIM] * dsk[:, e:e + 1] for e in range(HEADS_PER_GROUP)], axis=1)
        y_ref[0, :, g * GROUP_X:(g + 1) * GROUP_X] = y8[0:1, :] + dx


def _ssm_step(xbc, dt, conv_state, ssm_states, layer, conv_w, conv_b, dt_bias, a_log, d_skip):
    n = xbc.shape[0]
    par = _group_lanes
    st_spec = pl.BlockSpec((1, SSM_HEADS, SSM_HEAD_DIM, SSM_D_STATE), lambda s: (s, 0, 0, 0))
    st_in_specs = [pl.BlockSpec((None, 1, HEADS_PER_GROUP, SSM_HEAD_DIM, SSM_D_STATE),
                                lambda s, g=g: (layer, s, g, 0, 0)) for g in range(SSM_GROUPS)]
    y, st = pl.pallas_call(
        _ssm_step_kernel,
        out_shape=[jax.ShapeDtypeStruct((n, 1, D_INNER), F32),
                   jax.ShapeDtypeStruct(ssm_states.shape[1:], F32)],
        grid=(n,),
        in_specs=[pl.BlockSpec((1, 1, SSM_CONV_DIM), lambda s: (s, 0, 0)),
                  pl.BlockSpec((1, SSM_CONV_W - 1, SSM_CONV_DIM), lambda s: (s, 0, 0)),
                  pl.BlockSpec((1, 1, SSM_GROUPS * LANES), lambda s: (s, 0, 0)),
                  _resident((SSM_CONV_W, SSM_CONV_DIM)),
                  _resident((1, SSM_CONV_DIM)),
                  _resident((SSM_GROUPS, 1, LANES)),
                  _resident((SSM_GROUPS, 1, LANES)),
                  _resident((SSM_GROUPS, 1, LANES)),
                  *st_in_specs],
        out_specs=[pl.BlockSpec((1, 1, D_INNER), lambda s: (s, 0, 0)), st_spec],
        compiler_params=_cparams("parallel"),
        name="ssm_step",
    )(xbc.reshape(n, 1, SSM_CONV_DIM), conv_state, dt.reshape(n, 1, SSM_GROUPS * LANES), conv_w,
      conv_b.reshape(1, SSM_CONV_DIM), par(dt_bias), par(a_log), par(d_skip), *([ssm_states] * SSM_GROUPS))
    return y.reshape(n, D_INNER), st


MOBA_HEADS_PER_STEP = 4


def _moba_kernel(q_ref, k_ref, vt_ref, mean_ref, o_ref, bias_ref):
    i = pl.program_id(2)
    blk = MOBA_BLOCK
    n_blk = mean_ref.shape[1]
    n_heads = q_ref.shape[2] // HEAD_DIM
    sub = lax.broadcasted_iota(jnp.int32, (n_blk, blk), 0)
    sub_f = sub.astype(F32)
    key = lax.broadcasted_iota(jnp.int32, (blk, blk), 0)
    qry = lax.broadcasted_iota(jnp.int32, (blk, blk), 1)
    own = pl.multiple_of(i * blk, blk)
    heads = [slice(hd * HEAD_DIM, (hd + 1) * HEAD_DIM) for hd in range(n_heads)]

    qs_t = []
    for hd, lanes in enumerate(heads):
        q_t = q_ref[0, :, lanes].T
        gate = jnp.dot(mean_ref[0, :, lanes], q_t, precision=lax.Precision.HIGHEST,
                       preferred_element_type=F32)
        past = sub < i
        gate = jnp.where(past, gate, NEG_INF)
        bias = jnp.full((n_blk, blk), NEG_INF, F32)
        for _ in range(MOBA_TOPK):
            top = jnp.max(gate, axis=0, keepdims=True)
            first = jnp.min(jnp.where(gate == top, sub_f, float(n_blk)), axis=0, keepdims=True)
            pick = (sub_f == first) & past
            bias = jnp.where(pick, 0.0, bias)
            gate = jnp.where(pick, NEG_INF, gate)
        for j in range(n_blk):
            bias_ref[hd, j] = bias[j:j + 1, :]
        qs_t.append((q_t * (HEAD_DIM ** -0.5)).astype(BF16))

    def pair_scores(hd, j0):
        off = j0 * blk if isinstance(j0, int) else pl.multiple_of(j0 * blk, 2 * blk)
        s = _dot(k_ref[0, pl.ds(off, 2 * blk), heads[hd]], qs_t[hd])
        return s[:blk] + bias_ref[hd, j0], s[blk:] + bias_ref[hd, j0 + 1]

    s_own = [_dot(k_ref[0, pl.ds(own, blk), lanes], qs_t[hd]) for hd, lanes in enumerate(heads)]
    s_pair = [pair_scores(hd, 0) for hd in range(n_heads)]
    carry = []
    for hd, lanes in enumerate(heads):
        s = jnp.where(key <= qry, s_own[hd], NEG_INF)
        m = jnp.max(s, axis=0, keepdims=True)
        p = jnp.exp(s - m)
        l = jnp.sum(p, axis=0, keepdims=True)
        acc = _dot(vt_ref[0, i, lanes, :], p.astype(BF16))
        carry.append((m, l, acc) + s_pair[hd])

    def body(jj, carry):
        j0 = 2 * jj
        j_next = jnp.minimum(j0 + 2, n_blk - 2)
        s_next = [pair_scores(hd, j_next) for hd in range(n_heads)]
        out = []
        for hd, lanes in enumerate(heads):
            m, l, acc, s0, s1 = carry[hd]
            m_new = jnp.maximum(m, jnp.max(jnp.maximum(s0, s1), axis=0, keepdims=True))
            alpha = jnp.exp(m - m_new)
            p0 = jnp.exp(s0 - m_new)
            p1 = jnp.exp(s1 - m_new)
            l = alpha * l + jnp.sum(p0 + p1, axis=0, keepdims=True)
            v_pair = jnp.concatenate([vt_ref[0, j0, lanes, :], vt_ref[0, j0 + 1, lanes, :]], axis=1)
            p_pair = jnp.concatenate([p0.astype(BF16), p1.astype(BF16)], axis=0)
            acc = alpha * acc + _dot(v_pair, p_pair)
            out.append((m_new, l, acc) + s_next[hd])
        return tuple(out)

    carry = lax.fori_loop(0, (i + 1) // 2, body, tuple(carry))
    for hd, lanes in enumerate(heads):
        m, l, acc = carry[hd][:3]
        o_ref[0, :, lanes] = (acc / l).T.astype(o_ref.dtype)


def _moba_prompt(q, k16, v_t, kmean, n_seq, seq):
    n_blk = seq // MOBA_BLOCK
    width = MOBA_HEADS_PER_STEP * HEAD_DIM
    tile = pl.BlockSpec((1, MOBA_BLOCK, width), lambda n, h, i: (n, i, h))
    return pl.pallas_call(
        _moba_kernel,
        out_shape=jax.ShapeDtypeStruct((n_seq, seq, D_MODEL), BF16),
        grid=(n_seq, N_HEADS // MOBA_HEADS_PER_STEP, n_blk),
        in_specs=[tile,
                  pl.BlockSpec((1, seq, width), lambda n, h, i: (n, 0, h)),
                  pl.BlockSpec((1, n_blk, width, MOBA_BLOCK), lambda n, h, i: (n, 0, h, 0)),
                  pl.BlockSpec((1, n_blk, width), lambda n, h, i: (n, 0, h))],
        out_specs=tile,
        scratch_shapes=[pltpu.VMEM((MOBA_HEADS_PER_STEP, n_blk, 1, MOBA_BLOCK), F32)],
        compiler_params=_cparams("parallel", "parallel", "arbitrary"),
        name="moba_prompt",
    )(q, k16, v_t, kmean)


MEAN_BLOCKS_PER_STEP = 8


def _page_mean_kernel(pt_ref, *refs):
    del pt_ref
    o_ref = refs[-1]
    for b in range(MEAN_BLOCKS_PER_STEP):
        total = jnp.sum(refs[PAGES_PER_BLOCK * b][0], axis=0)
        for t in range(1, PAGES_PER_BLOCK):
            total = total + jnp.sum(refs[PAGES_PER_BLOCK * b + t][0], axis=0)
        o_ref[0, b] = total * (1.0 / MOBA_BLOCK)


def _page_means(cache_k, page_table):
    n_seq, n_pages = page_table.shape
    n_blk = n_pages // PAGES_PER_BLOCK
    per_step = MEAN_BLOCKS_PER_STEP * PAGES_PER_BLOCK
    page = lambda t: pl.BlockSpec(
        (1, PAGE_SIZE, N_HEADS, HEAD_DIM), lambda s, b, pt: (pt[s * n_pages + per_step * b + t], 0, 0, 0))
    out = pl.pallas_call(
        _page_mean_kernel,
        out_shape=jax.ShapeDtypeStruct((n_seq, n_blk, N_HEADS, HEAD_DIM), F32),
        grid_spec=pltpu.PrefetchScalarGridSpec(
            num_scalar_prefetch=1,
            grid=(n_seq, n_blk // MEAN_BLOCKS_PER_STEP),
            in_specs=[page(t) for t in range(per_step)],
            out_specs=pl.BlockSpec((1, MEAN_BLOCKS_PER_STEP, N_HEADS, HEAD_DIM), lambda s, b, pt: (s, b, 0, 0)),
        ),
        compiler_params=_cparams("parallel", "parallel"),
        name="page_means",
    )(page_table.reshape(-1), *([cache_k] * per_step))
    return out.reshape(n_seq, n_blk, D_MODEL)


def _select_kernel(q_ref, mean_ref, idx_ref):
    n_blk = mean_ref.shape[1]
    prod = mean_ref[0] * q_ref[0]
    lane = lax.broadcasted_iota(jnp.int32, (n_blk, LANES), 1)
    sub = lax.broadcasted_iota(jnp.int32, (n_blk, LANES), 0)
    gate = jnp.zeros((n_blk, LANES), F32)
    for hd in range(N_HEADS):
        g_h = jnp.sum(prod[:, hd * HEAD_DIM:(hd + 1) * HEAD_DIM], axis=1, keepdims=True)
        gate = jnp.where(lane == hd, g_h, gate)
    out_sub = lax.broadcasted_iota(jnp.int32, (SUBLANES, LANES), 0)
    out = jnp.zeros((SUBLANES, LANES), jnp.int32)
    for j in range(n_blk):
        g_j = gate[j:j + 1, :]
        beats = (gate > g_j) | ((gate == g_j) & (sub < j))
        rank = jnp.sum(jnp.where(beats, 1.0, 0.0), axis=0, keepdims=True)
        for r in range(MOBA_TOPK):
            out = jnp.where((out_sub == r) & (rank == float(r)), j, out)
    idx_ref[0] = out


def _select_blocks(q, means):
    n_seq, n_blk, _ = means.shape
    idx = pl.pallas_call(
        _select_kernel,
        out_shape=jax.ShapeDtypeStruct((n_seq, SUBLANES, LANES), jnp.int32),
        grid=(n_seq,),
        in_specs=[pl.BlockSpec((1, 1, D_MODEL), lambda s: (s, 0, 0)),
                  pl.BlockSpec((1, n_blk, D_MODEL), lambda s: (s, 0, 0))],
        out_specs=pl.BlockSpec((1, SUBLANES, LANES), lambda s: (s, 0, 0)),
        compiler_params=_cparams("parallel"),
        name="select_blocks",
    )(q.reshape(n_seq, 1, D_MODEL), means)
    return idx[:, :MOBA_TOPK, :N_HEADS]


N_SEL_PAGES = MOBA_TOPK * PAGES_PER_BLOCK


def _decode_attn_kernel(pt_ref, idx_ref, q_ref, kn_ref, vn_ref, k_hbm, v_hbm, o_ref, kbuf, vbuf, sems,
                        *, n_pages):
    s_i = pl.program_id(0)
    n_seq = pl.num_programs(0)
    buf = s_i % 2

    def copies(seq, hd, slot):
        r, t = divmod(slot, PAGES_PER_BLOCK)
        blk = idx_ref[(seq * MOBA_TOPK + r) * N_HEADS + hd]
        pg = pt_ref[seq * n_pages + PAGES_PER_BLOCK * blk + t]
        half = seq % 2
        return (pltpu.make_async_copy(k_hbm.at[pg, :, hd, :], kbuf.at[half, hd, slot], sems.at[half, 0, hd, slot]),
                pltpu.make_async_copy(v_hbm.at[pg, :, hd, :], vbuf.at[half, hd, slot], sems.at[half, 1, hd, slot]))

    def start_all(seq):
        for hd in range(N_HEADS):
            for slot in range(N_SEL_PAGES):
                for cp in copies(seq, hd, slot):
                    cp.start()

    @pl.when(s_i == 0)
    def _():
        start_all(s_i)

    @pl.when(s_i + 1 < n_seq)
    def _():
        start_all(s_i + 1)

    for hd in range(N_HEADS):
        lanes = slice(hd * HEAD_DIM, (hd + 1) * HEAD_DIM)
        for slot in range(N_SEL_PAGES):
            for cp in copies(s_i, hd, slot):
                cp.wait()
        q = q_ref[0, :, lanes] * (HEAD_DIM ** -0.5)
        s_own = jnp.sum(q * kn_ref[0, :, lanes], axis=1, keepdims=True)
        scores = [jnp.sum(kbuf[buf, hd, slot] * q, axis=1, keepdims=True) for slot in range(N_SEL_PAGES)]
        m = s_own
        for s in scores:
            m = jnp.maximum(m, jnp.max(s, axis=0, keepdims=True))
        p_own = jnp.exp(s_own - m)
        l = p_own
        acc = p_own * vn_ref[0, :, lanes]
        for slot, s in enumerate(scores):
            p = jnp.exp(s - m)
            l = l + jnp.sum(p, axis=0, keepdims=True)
            acc = acc + jnp.sum(p * vbuf[buf, hd, slot], axis=0, keepdims=True)
        o_ref[0, :, lanes] = acc / l


def _decode_attn(q, k_new, v_new, cache_k, cache_v, page_table, idx):
    n_seq, n_pages = page_table.shape
    vec = pl.BlockSpec((1, 1, D_MODEL), lambda s, pt, sel: (s, 0, 0))
    hbm = pl.BlockSpec(memory_space=pl.ANY)
    r3 = lambda a: a.reshape(n_seq, 1, D_MODEL)
    slabs = (2, N_HEADS, N_SEL_PAGES, PAGE_SIZE, HEAD_DIM)
    out = pl.pallas_call(
        functools.partial(_decode_attn_kernel, n_pages=n_pages),
        out_shape=jax.ShapeDtypeStruct((n_seq, 1, D_MODEL), F32),
        grid_spec=pltpu.PrefetchScalarGridSpec(
            num_scalar_prefetch=2,
            grid=(n_seq,),
            in_specs=[vec, vec, vec, hbm, hbm],
            out_specs=vec,
            scratch_shapes=[pltpu.VMEM(slabs, F32), pltpu.VMEM(slabs, F32),
                            pltpu.SemaphoreType.DMA((2, 2, N_HEADS, N_SEL_PAGES))],
        ),
        compiler_params=_cparams("arbitrary"),
        name="decode_attn",
    )(page_table.reshape(-1), idx.reshape(-1), r3(q), r3(k_new), r3(v_new), cache_k, cache_v)
    return out.reshape(n_seq, D_MODEL)


def _cast_kernel(*refs, n_parts):
    w_refs, o_refs = refs[:-n_parts], refs[-n_parts:]
    streams = len(w_refs) // n_parts
    rows = w_refs[0].shape[0]
    for p, o_ref in enumerate(o_refs):
        for r in range(streams):
            o_ref[r * rows:(r + 1) * rows, :] = w_refs[p * streams + r][...].astype(o_ref.dtype)


CAST_STREAMS = 4


def _to_bf16(w, layer, row_tile, col_parts=1):
    _, rows, cols = w.shape
    streams = CAST_STREAMS if rows % (CAST_STREAMS * row_tile) == 0 else 1
    width = cols // col_parts
    in_specs = [pl.BlockSpec((None, row_tile, width), lambda i, p=p, r=r: (layer, i * streams + r, p))
                for p in range(col_parts) for r in range(streams)]
    out = pl.pallas_call(
        functools.partial(_cast_kernel, n_parts=col_parts),
        out_shape=[jax.ShapeDtypeStruct((rows, width), BF16)] * col_parts,
        grid=(rows // (streams * row_tile),),
        in_specs=in_specs,
        out_specs=[pl.BlockSpec((streams * row_tile, width), lambda i: (i, 0))] * col_parts,
        compiler_params=_cparams("parallel"),
        name="to_bf16",
    )(*([w] * len(in_specs)))
    return out[0] if col_parts == 1 else tuple(out)


def _dt_weight(w_in):
    w_dt = w_in[:, D_INNER + SSM_CONV_DIM:].reshape(D_MODEL, SSM_GROUPS, HEADS_PER_GROUP)
    w_dt = jnp.pad(w_dt, ((0, 0), (0, 0), (0, LANES - HEADS_PER_GROUP)))
    return _to_bf16(w_dt.reshape(1, D_MODEL, SSM_GROUPS * LANES), 0, D_MODEL)


W_ROW_TILE = 32
W_DOWN_ROW_TILE = D_FF // 16


def kernel(x_prompt, x_sample, state_conv, state_ssm, cache_k, cache_v, page_table, norm_ffn_a, w_ffn_a_up, w_ffn_a_down, norm_mix, norm_ffn_b, w_ffn_b_up, w_ffn_b_down, ssm_w_in, ssm_conv_w, ssm_conv_b, ssm_dt_bias, ssm_a_log, ssm_d, ssm_norm, ssm_w_out, norm_kv, w_kv, w_q, w_o, norm_final):
    n_p, seq, d = x_prompt.shape
    n_s = x_sample.shape[0]
    past_len = page_table.shape[1] * PAGE_SIZE
    tm_p = 512
    tm_s = n_s

    hp = x_prompt.reshape(n_p * seq, d)
    hs = x_sample.reshape(n_s, d)
    tab_p = _rope_tables(jnp.arange(seq, dtype=jnp.int32))
    tab_s = _rope_tables(jnp.full((n_s,), past_len, jnp.int32))

    conv_p, ssm_p, conv_s, ssm_s = [], [], [], []
    k_p = v_p = k_s = v_s = None
    for layer in range(DEPTH):
        if layer == N_A_LAYERS:
            w_kv16 = _to_bf16(w_kv[None], 0, W_ROW_TILE)
            k_p, v_p, kmean_p, k16, v_t = _kv_proj(hp, norm_kv, w_kv16, tab_p, MOBA_BLOCK, True)
            k_s, v_s = _kv_proj(hs, norm_kv, w_kv16, tab_s, tm_s, False)
            k16 = k16.reshape(n_p, seq, d)
            v_t = v_t.reshape(n_p, seq // MOBA_BLOCK, d, MOBA_BLOCK)
            kmean_p = kmean_p.reshape(n_p, seq // MOBA_BLOCK, d)
            kmean_s = _page_means(cache_k, page_table)

        w_up = _to_bf16(w_ffn_a_up, layer, W_ROW_TILE, 2)
        w_dn = _to_bf16(w_ffn_a_down, layer, W_DOWN_ROW_TILE, 2)
        hp = _ffn(hp, norm_ffn_a[layer], w_up, w_dn, tm_p)
        hs = _ffn(hs, norm_ffn_a[layer], w_up, w_dn, tm_s)

        if layer < N_A_LAYERS:
            w_in = _to_bf16(ssm_w_in, layer, W_ROW_TILE)
            w_dt = _dt_weight(ssm_w_in[layer])
            w_out = _to_bf16(ssm_w_out, layer, W_ROW_TILE)
            ssm_args = (ssm_conv_w[layer], ssm_conv_b[layer], ssm_dt_bias[layer], ssm_a_log[layer], ssm_d[layer])
            z, xbc, dt, tail = _in_proj(hp, norm_mix[layer], w_in, w_dt, ssm_conv_w[layer], ssm_conv_b[layer],
                                        tm_p, seq)
            y, st = _ssd_prompt(xbc, dt, *ssm_args[2:], n_p, seq)
            conv_p.append(tail[:, SUBLANES - (SSM_CONV_W - 1):])
            ssm_p.append(st)
            mix_p = (y.reshape(n_p * seq, D_INNER), w_out, z, ssm_norm[layer])

            z, xbc = _proj(hs, norm_mix[layer], w_in, (D_INNER, SSM_CONV_DIM), tm_s)
            dt, = _proj(hs, norm_mix[layer], w_dt, (SSM_GROUPS * LANES,), tm_s)
            y, st = _ssm_step(xbc, dt, state_conv[layer], state_ssm, layer, *ssm_args)
            conv_s.append(jnp.concatenate([state_conv[layer][:, 1:], xbc[:, None, :]], axis=1))
            ssm_s.append(st)
            mix_s = (y, w_out, z, ssm_norm[layer])
        else:
            j = layer - N_A_LAYERS
            w_q16, w_o16 = _to_bf16(w_q, j, W_ROW_TILE), _to_bf16(w_o, j, W_ROW_TILE)
            q = _q_proj(hp, norm_mix[layer], w_q16, tab_p, tm_p)
            o = _moba_prompt(q.reshape(n_p, seq, d), k16, v_t, kmean_p, n_p, seq)
            mix_p = (o.reshape(n_p * seq, d), w_o16)

            q = _q_proj(hs, norm_mix[layer], w_q16, tab_s, tm_s)
            idx = _select_blocks(q, kmean_s)
            o = _decode_attn(q, k_s, v_s, cache_k, cache_v, page_table, idx)
            mix_s = (o, w_o16)

        w_up = _to_bf16(w_ffn_b_up, layer, W_ROW_TILE, 2)
        w_dn = _to_bf16(w_ffn_b_down, layer, W_DOWN_ROW_TILE, 2)
        final_g = norm_final if layer == DEPTH - 1 else None
        hp = _ffn(hp, norm_ffn_b[layer], w_up, w_dn, tm_p, final_g, mix_p)
        hs = _ffn(hs, norm_ffn_b[layer], w_up, w_dn, tm_s, final_g, mix_s)

    heads = (N_HEADS, HEAD_DIM)
    return (hp.reshape(n_p, seq, d), hs.reshape(n_s, 1, d),
            jnp.stack(conv_p), jnp.stack(ssm_p),
            k_p.reshape(n_p, seq, *heads), v_p.reshape(n_p, seq, *heads),
            jnp.stack(conv_s), jnp.stack(ssm_s),
            k_s.reshape(n_s, 1, *heads), v_s.reshape(n_s, 1, *heads))
```
